```python
import math
import jax, jax.numpy as jnp
from jax import lax
import numpy as np

D_MODEL = 1024
BATCH = 4
SEQ = 8192
DEPTH = 2

N_MIXERS = 2
N_ATTN_LAYERS = (DEPTH + N_MIXERS - 1) // N_MIXERS
N_SSM_LAYERS = DEPTH // N_MIXERS
DILATION_PAIRS = ((128, 1), (512, 4), (2048, 16))
N_GROUPS = len(DILATION_PAIRS)
HEADS_PER_GROUP = 8
HEAD_DIM = 128
ROPE_THETA = 10000.0
ATTN_IN_WIDTH = N_GROUPS * 3 * HEADS_PER_GROUP * HEAD_DIM
ATTN_OUT_WIDTH = HEADS_PER_GROUP * HEAD_DIM
D_SSM = D_MODEL
SSM_GROUP_CH = 16
SSM_GROUPS = D_SSM // SSM_GROUP_CH
SSM_STATE = 64
D_FF = 2816
CONV_WIDTH = 3
DEEPNORM_ALPHA = (2.0 * DEPTH) ** 0.25
DEEPNORM_BETA = (8.0 * DEPTH) ** -0.25
LN_EPS = 1e-5
NEG_INF = -1e30

kernel_name = "hybrid_dilated_attn_s5_convffn_deepnorm"


def layer_norm(x, g, b):
    xf = x.astype(jnp.float32)
    mu = jnp.mean(xf, axis=-1, keepdims=True)
    var = jnp.mean(jnp.square(xf - mu), axis=-1, keepdims=True)
    y = (xf - mu) * lax.rsqrt(var + LN_EPS) * g.astype(jnp.float32) + b.astype(jnp.float32)
    return y.astype(x.dtype)


def rotary(t, cos, sin):
    t1, t2 = jnp.split(t, 2, axis=-1)
    return jnp.concatenate([t1 * cos - t2 * sin, t2 * cos + t1 * sin], axis=-1)


def dilated_window_group(q, k, v, window, dilation):
    B_, S_, H, E = q.shape
    steps = window // dilation
    blk = steps
    chunk = dilation * blk
    s_pad = -(-S_ // chunk) * chunk
    nb = s_pad // chunk
    L = s_pad // dilation

    def to_blocks(t):
        t = jnp.pad(t, ((0, 0), (0, s_pad - S_), (0, 0), (0, 0)))
        t = t.reshape(B_, L, dilation, H, E).transpose(0, 2, 1, 3, 4)
        return t.reshape(B_, dilation, nb, blk, H, E)

    def with_prev(t):
        prev = jnp.pad(t[:, :, :-1], ((0, 0), (0, 0), (1, 0), (0, 0), (0, 0), (0, 0)))
        return jnp.concatenate([prev, t], axis=3)

    qb = to_blocks(q)
    kk = with_prev(to_blocks(k))
    vv = with_prev(to_blocks(v))
    scores = jnp.einsum('bdnqhe,bdnkhe->bdnhqk', qb, kk,
                        preferred_element_type=jnp.float32)
    qi = jnp.arange(blk)[:, None]
    kj = jnp.arange(2 * blk)[None, :]
    rel = qi + blk - kj
    band = (rel >= 0) & (rel <= steps)
    first = (jnp.arange(nb)[:, None, None] == 0) & (kj[None] < blk)
    valid = (band[None] & ~first)[None, None, :, None]
    scores = jnp.where(valid, scores, NEG_INF)
    m = jnp.max(scores, axis=-1, keepdims=True)
    p = jnp.exp(scores - m)
    l = jnp.sum(p, axis=-1, keepdims=True)
    o = jnp.einsum('bdnhqk,bdnkhe->bdnqhe', p, vv.astype(jnp.float32))
    o = o / l.transpose(0, 1, 2, 4, 3, 5)
    lse = (m + jnp.log(l))[..., 0].transpose(0, 1, 2, 4, 3)
    o = o.reshape(B_, dilation, L, H, E).transpose(0, 2, 1, 3, 4).reshape(B_, s_pad, H, E)[:, :S_]
    lse = lse.reshape(B_, dilation, L, H).transpose(0, 2, 1, 3).reshape(B_, s_pad, H)[:, :S_]
    return o, lse


def dilated_attention(h, w_in, w_out):
    B_, S_, _ = h.shape
    qkv = (h @ w_in).reshape(B_, S_, N_GROUPS, 3, HEADS_PER_GROUP, HEAD_DIM)
    pos = jnp.arange(S_, dtype=jnp.float32)
    inv_freq = ROPE_THETA ** (-jnp.arange(0, HEAD_DIM, 2, dtype=jnp.float32) / HEAD_DIM)
    ang = pos[:, None] * inv_freq[None, :]
    cos = jnp.cos(ang)[None, :, None, None, :].astype(h.dtype)
    sin = jnp.sin(ang)[None, :, None, None, :].astype(h.dtype)
    q = rotary(qkv[:, :, :, 0], cos, sin) * (HEAD_DIM ** -0.5)
    k = rotary(qkv[:, :, :, 1], cos, sin)
    v = qkv[:, :, :, 2]
    outs, lses = [], []
    for g, (window, dilation) in enumerate(DILATION_PAIRS):
        o, lse = dilated_window_group(q[:, :, g], k[:, :, g], v[:, :, g], window, dilation)
        outs.append(o)
        lses.append(lse)
    w = jax.nn.softmax(jnp.stack(lses, axis=0), axis=0)
    o = jnp.sum(w[..., None] * jnp.stack(outs, axis=0), axis=0)
    return o.reshape(B_, S_, ATTN_OUT_WIDTH).astype(h.dtype) @ w_out


def _ssm_combine(e1, e2):
    a1r, a1i, b1r, b1i = e1
    a2r, a2i, b2r, b2i = e2
    return (a2r * a1r - a2i * a1i,
            a2r * a1i + a2i * a1r,
            a2r * b1r - a2i * b1i + b2r,
            a2r * b1i + a2i * b1r + b2i)


def s5_mixer(h, w_in, a_re, a_im, log_dt, b_re, b_im, c_re, c_im, d_skip, w_glu, b_glu, w_out):
    f32 = jnp.float32
    B_, S_, _ = h.shape
    u = (h @ w_in).astype(f32)
    ug = u.reshape(B_, S_, SSM_GROUPS, SSM_GROUP_CH)
    a_re = a_re.astype(f32)
    a_im = a_im.astype(f32)
    dt = jnp.exp(log_dt.astype(f32))[:, None]
    mag = jnp.exp(a_re * dt)
    lam_re = mag * jnp.cos(a_im * dt)
    lam_im = mag * jnp.sin(a_im * dt)
    nr, ni = lam_re - 1.0, lam_im
    den = a_re * a_re + a_im * a_im
    coef_re = ((nr * a_re + ni * a_im) / den)[..., None]
    coef_im = ((ni * a_re - nr * a_im) / den)[..., None]
    b_re = b_re.astype(f32)
    b_im = b_im.astype(f32)
    bb_re = coef_re * b_re - coef_im * b_im
    bb_im = coef_re * b_im + coef_im * b_re
    bu_re = jnp.einsum('bsgc,gnc->bsgn', ug, bb_re)
    bu_im = jnp.einsum('bsgc,gnc->bsgn', ug, bb_im)
    lr = jnp.broadcast_to(lam_re, (1, S_, SSM_GROUPS, SSM_STATE))
    li = jnp.broadcast_to(lam_im, (1, S_, SSM_GROUPS, SSM_STATE))
    _, _, xr, xi = lax.associative_scan(_ssm_combine, (lr, li, bu_re, bu_im), axis=1)
    y = (jnp.einsum('bsgn,gcn->bsgc', xr, c_re.astype(f32))
         - jnp.einsum('bsgn,gcn->bsgc', xi, c_im.astype(f32)))
    y = y.reshape(B_, S_, D_SSM) + d_skip.astype(f32) * u
    z = jax.nn.gelu(y)
    z = z * jax.nn.sigmoid(z @ w_glu.astype(f32) + b_glu.astype(f32))
    return z.astype(h.dtype) @ w_out


def conv_ffn(h, w_up, conv_w, conv_b, w_down):
    S_ = h.shape[1]
    gate, val = jnp.split(h @ w_up, 2, axis=-1)
    gp = jnp.pad(gate, ((0, 0), (CONV_WIDTH - 1, 0), (0, 0)))
    conv = conv_b
    for t in range(CONV_WIDTH):
        conv = conv + conv_w[t] * gp[:, t:t + S_]
    return (jax.nn.silu(conv) * val) @ w_down


def setup_inputs(seed: int = 0) -> dict:
    key = jax.random.key(seed)
    ks = jax.random.split(key, 20)
    f32 = jnp.float32

    def nrm(k, shape, scale):
        return jax.random.normal(k, shape, f32) * scale

    nA, nB = N_ATTN_LAYERS, N_SSM_LAYERS
    G, N, C = SSM_GROUPS, SSM_STATE, SSM_GROUP_CH
    n_idx = jnp.arange(N, dtype=f32)
    return {
        "x": nrm(ks[0], (BATCH, SEQ, D_MODEL), 1.0),
        "attn_w_in": nrm(ks[1], (nA, D_MODEL, ATTN_IN_WIDTH), D_MODEL ** -0.5),
        "attn_w_out": nrm(ks[2], (nA, ATTN_OUT_WIDTH, D_MODEL), ATTN_OUT_WIDTH ** -0.5 * DEEPNORM_BETA),
        "ssm_w_in": nrm(ks[3], (nB, D_MODEL, D_SSM), D_MODEL ** -0.5),
        "ssm_a_re": -0.5 + nrm(ks[4], (nB, G, N), 0.01),
        "ssm_a_im": jnp.tile((math.pi * n_idx)[None, None, :], (nB, G, 1)),
        "ssm_log_dt": jax.random.uniform(ks[5], (nB, G), f32, math.log(1e-3), math.log(1e-1)),
        "ssm_b_re": nrm(ks[6], (nB, G, N, C), (2.0 * C) ** -0.5),
        "ssm_b_im": nrm(ks[7], (nB, G, N, C), (2.0 * C) ** -0.5),
        "ssm_c_re": nrm(ks[8], (nB, G, C, N), N ** -0.5),
        "ssm_c_im": nrm(ks[9], (nB, G, C, N), N ** -0.5),
        "ssm_d": nrm(ks[10], (nB, D_SSM), 1.0),
        "ssm_w_glu": nrm(ks[11], (nB, D_SSM, D_SSM), D_SSM ** -0.5),
        "ssm_b_glu": nrm(ks[12], (nB, D_SSM), 0.02),
        "ssm_w_out": nrm(ks[13], (nB, D_SSM, D_MODEL), D_SSM ** -0.5 * DEEPNORM_BETA),
        "ffn_w_up": nrm(ks[14], (DEPTH, D_MODEL, 2 * D_FF), D_MODEL ** -0.5),
        "ffn_conv_w": nrm(ks[15], (DEPTH, CONV_WIDTH, D_FF), CONV_WIDTH ** -0.5),
        "ffn_conv_b": nrm(ks[16], (DEPTH, D_FF), 0.02),
        "ffn_w_down": nrm(ks[17], (DEPTH, D_FF, D_MODEL), D_FF ** -0.5 * DEEPNORM_BETA),
        "ln_g": 1.0 + nrm(ks[18], (DEPTH, 2, D_MODEL), 0.02),
        "ln_b": nrm(ks[19], (DEPTH, 2, D_MODEL), 0.02),
    }


def reference(x, attn_w_in, attn_w_out, ssm_w_in, ssm_a_re, ssm_a_im, ssm_log_dt,
              ssm_b_re, ssm_b_im, ssm_c_re, ssm_c_im, ssm_d, ssm_w_glu, ssm_b_glu,
              ssm_w_out, ffn_w_up, ffn_conv_w, ffn_conv_b, ffn_w_down, ln_g, ln_b):
    for i in range(DEPTH):
        j = i // N_MIXERS
        if i % N_MIXERS == 0:
            y = dilated_attention(x, attn_w_in[j], attn_w_out[j])
        else:
            y = s5_mixer(x, ssm_w_in[j], ssm_a_re[j], ssm_a_im[j], ssm_log_dt[j],
                         ssm_b_re[j], ssm_b_im[j], ssm_c_re[j], ssm_c_im[j], ssm_d[j],
                         ssm_w_glu[j], ssm_b_glu[j], ssm_w_out[j])
        x = layer_norm(DEEPNORM_ALPHA * x + y, ln_g[i, 0], ln_b[i, 0])
        f = conv_ffn(x, ffn_w_up[i], ffn_conv_w[i], ffn_conv_b[i], ffn_w_down[i])
        x = layer_norm(DEEPNORM_ALPHA * x + f, ln_g[i, 1], ln_b[i, 1])
    return x
```

```python
import functools
import math

import jax
import jax.numpy as jnp
from jax import lax
from jax.experimental import pallas as pl
from jax.experimental.pallas import tpu as pltpu

F32 = jnp.float32
BF16 = jnp.bfloat16

DEPTH = 2
DILATION_PAIRS = ((128, 1), (512, 4), (2048, 16))
N_GROUPS = len(DILATION_PAIRS)
HEADS = 8
HEAD_DIM = 128
ROPE_THETA = 10000.0
SSM_GROUP_CH = 16
SSM_STATE = 64
CONV_WIDTH = 3
DEEPNORM_ALPHA = (2.0 * DEPTH) ** 0.25
LN_EPS = 1e-5
NEG_INF = -1e30

LANES = 128
SUBLANES = 8
VMEM_LIMIT_BYTES = 56 * 1024 * 1024

ATTN_BLOCK = 128
SSM_CHUNK = 128


def _params(*semantics):
    return pltpu.CompilerParams(dimension_semantics=semantics,
                                vmem_limit_bytes=VMEM_LIMIT_BYTES)


def _layer_norm(r, g, b):
    mu = jnp.mean(r, axis=-1, keepdims=True)
    d = r - mu
    var = jnp.mean(d * d, axis=-1, keepdims=True)
    return d * lax.rsqrt(var + LN_EPS) * g + b


def _dot(a, b):
    return jnp.dot(a, b, preferred_element_type=F32)


def _dot_nt(a, b):
    return lax.dot_general(a, b, (((1,), (1,)), ((), ())), preferred_element_type=F32)


def _dot_tn(a, b):
    return lax.dot_general(a, b, (((0,), (0,)), ((), ())), preferred_element_type=F32)


def _qkv_kernel(x_ref, w_ref, cos_ref, sin_ref, o_ref, xb_ref):
    j = pl.program_id(1)

    @pl.when(j == 0)
    def _():
        xb_ref[...] = x_ref[...].astype(BF16)

    acc = _dot(xb_ref[...], w_ref[...])
    kind = j % 3

    @pl.when(kind == 2)
    def _():
        o_ref[...] = acc.astype(BF16)

    @pl.when(kind != 2)
    def _():
        scale = jnp.where(kind == 0, HEAD_DIM ** -0.5, 1.0).astype(F32)
        cos = cos_ref[...]
        sin = sin_ref[...]
        for h in range(HEADS):
            t = acc[:, h * HEAD_DIM:(h + 1) * HEAD_DIM]
            r = t * cos + pltpu.roll(t, HEAD_DIM // 2, axis=1) * sin
            o_ref[:, h * HEAD_DIM:(h + 1) * HEAD_DIM] = (r * scale).astype(BF16)


def _qkv_rope(x2d, w_in_bf16, cos_tab, sin_tab, seq, tm):
    tokens, d_model = x2d.shape
    width = w_in_bf16.shape[1]
    tn = HEADS * HEAD_DIM
    seq_tiles = seq // tm
    return pl.pallas_call(
        _qkv_kernel,
        out_shape=jax.ShapeDtypeStruct((tokens, width), BF16),
        grid=(tokens // tm, width // tn),
        in_specs=[
            pl.BlockSpec((tm, d_model), lambda i, j: (i, 0)),
            pl.BlockSpec((d_model, tn), lambda i, j: (0, j)),
            pl.BlockSpec((tm, HEAD_DIM), lambda i, j: (i % seq_tiles, 0)),
            pl.BlockSpec((tm, HEAD_DIM), lambda i, j: (i % seq_tiles, 0)),
        ],
        out_specs=pl.BlockSpec((tm, tn), lambda i, j: (i, j)),
        scratch_shapes=[pltpu.VMEM((tm, d_model), BF16)],
        compiler_params=_params("parallel", "arbitrary"),
        name="qkv_rope",
    )(x2d, w_in_bf16, cos_tab, sin_tab)


def _attn_kernel(q_ref, k_ref, v_ref, kp_ref, vp_ref, o_ref, lse_ref, *, nbq):
    m = pl.program_id(2)
    row = lax.broadcasted_iota(jnp.int32, (ATTN_BLOCK, ATTN_BLOCK), 0)
    col = lax.broadcasted_iota(jnp.int32, (ATTN_BLOCK, ATTN_BLOCK), 1)
    cur_ok = col <= row
    prev_tri = col >= row
    has_prev = m > 0
    for blk in range(nbq):
        rows = slice(blk * ATTN_BLOCK, (blk + 1) * ATTN_BLOCK)
        prows = slice((blk - 1) * ATTN_BLOCK, blk * ATTN_BLOCK)
        prev_ok = jnp.logical_and(prev_tri, has_prev) if blk == 0 else prev_tri
        lse_tile = jnp.zeros((ATTN_BLOCK, LANES), F32)
        for h in range(HEADS):
            cols = slice(h * HEAD_DIM, (h + 1) * HEAD_DIM)
            q = q_ref[rows, cols]
            kc = k_ref[rows, cols]
            vc = v_ref[rows, cols]
            if blk == 0:
                kp = kp_ref[:, cols]
                vp = vp_ref[:, cols]
            else:
                kp = k_ref[prows, cols]
                vp = v_ref[prows, cols]
            s_c = jnp.where(cur_ok, _dot_nt(q, kc), NEG_INF)
            s_p = jnp.where(prev_ok, _dot_nt(q, kp), NEG_INF)
            mx = jnp.maximum(jnp.max(s_c, axis=1, keepdims=True),
                             jnp.max(s_p, axis=1, keepdims=True))
            p_c = jnp.exp(s_c - mx)
            p_p = jnp.exp(s_p - mx)
            l = jnp.sum(p_c, axis=1, keepdims=True) + jnp.sum(p_p, axis=1, keepdims=True)
            o = _dot(p_c.astype(BF16), vc) + _dot(p_p.astype(BF16), vp)
            o_ref[rows, cols] = (o / l).astype(BF16)
            lse_tile = jnp.where(col == h, mx + jnp.log(l), lse_tile)
        lse_ref[rows, :] = lse_tile


def _dilated_attention_group(qkv, g, dilation, batch, seq):
    hw = HEADS * HEAD_DIM
    width = qkv.shape[1]
    sub_len = seq // dilation
    nbq = min(4, sub_len // ATTN_BLOCK)
    rows = nbq * ATTN_BLOCK
    col_blocks = width // hw
    view = qkv.reshape(batch, sub_len, dilation * width)
    base = g * 3

    def main_spec(kind):
        return pl.BlockSpec((None, rows, hw),
                            lambda b, r, m: (b, m, r * col_blocks + base + kind))

    def prev_spec(kind):
        return pl.BlockSpec((None, ATTN_BLOCK, hw),
                            lambda b, r, m: (b, jnp.maximum(m * nbq - 1, 0),
                                             r * col_blocks + base + kind))

    o, lse = pl.pallas_call(
        functools.partial(_attn_kernel, nbq=nbq),
        out_shape=(jax.ShapeDtypeStruct((batch, sub_len, dilation * hw), BF16),
                   jax.ShapeDtypeStruct((batch, sub_len, dilation * LANES), F32)),
        grid=(batch, dilation, sub_len // rows),
        in_specs=[main_spec(0), main_spec(1), main_spec(2), prev_spec(1), prev_spec(2)],
        out_specs=(pl.BlockSpec((None, rows, hw), lambda b, r, m: (b, m, r)),
                   pl.BlockSpec((None, rows, LANES), lambda b, r, m: (b, m, r))),
        compiler_params=_params("parallel", "parallel", "arbitrary"),
        name=f"dilated_attn_g{g}",
    )(view, view, view, view, view)
    return o.reshape(batch * seq, hw), lse.reshape(batch * seq, LANES)


def _merge_out_kernel(o0_ref, o1_ref, o2_ref, l0_ref, l1_ref, l2_ref, e_ref, x_ref, w_ref,
                      g_ref, b_ref, out_ref):
    lses = (l0_ref[...], l1_ref[...], l2_ref[...])
    mx = jnp.maximum(jnp.maximum(lses[0], lses[1]), lses[2])
    ws = [jnp.exp(l - mx) for l in lses]
    inv = 1.0 / (ws[0] + ws[1] + ws[2])
    e = e_ref[...]
    merged = None
    for w, o_ref in zip(ws, (o0_ref, o1_ref, o2_ref)):
        w = w * inv
        w_hi = w.astype(BF16)
        w_lo = (w - w_hi.astype(F32)).astype(BF16)
        w_full = _dot(w_hi, e) + _dot(w_lo, e)
        term = w_full * o_ref[...].astype(F32)
        merged = term if merged is None else merged + term
    y = _dot(merged.astype(BF16), w_ref[...])
    r = DEEPNORM_ALPHA * x_ref[...] + y
    out_ref[...] = _layer_norm(r, g_ref[...], b_ref[...])


def _merge_out(os, lses, x2d, w_out_bf16, ln_g, ln_b, tm):
    tokens, d_model = x2d.shape
    hw = HEADS * HEAD_DIM
    expand = (jnp.arange(LANES)[:, None] == (jnp.arange(hw)[None, :] // HEAD_DIM)).astype(BF16)
    row = lambda i: (i, 0)
    fixed = lambda i: (0, 0)
    return pl.pallas_call(
        _merge_out_kernel,
        out_shape=jax.ShapeDtypeStruct((tokens, d_model), F32),
        grid=(tokens // tm,),
        in_specs=[pl.BlockSpec((tm, hw), row)] * 3 + [pl.BlockSpec((tm, LANES), row)] * 3 + [
            pl.BlockSpec((LANES, hw), fixed),
            pl.BlockSpec((tm, d_model), row),
            pl.BlockSpec((hw, d_model), fixed),
            pl.BlockSpec((1, d_model), fixed),
            pl.BlockSpec((1, d_model), fixed),
        ],
        out_specs=pl.BlockSpec((tm, d_model), row),
        compiler_params=_params("parallel"),
        name="attn_merge_out_ln",
    )(*os, *lses, expand, x2d, w_out_bf16, ln_g.reshape(1, -1), ln_b.reshape(1, -1))


def _ffn_kernel(x_ref, halo_ref, wup_ref, cw_ref, cb_ref, wdown_ref, g_ref, b_ref, out_ref,
                xe_ref, h_ref, *, seq_tiles, d_ff, chunk):
    i = pl.program_id(0)
    tm = x_ref.shape[0]
    x = x_ref[...]
    first = (i % seq_tiles) == 0
    xe_ref[0:SUBLANES, :] = jnp.where(first, 0.0, halo_ref[...]).astype(BF16)
    xe_ref[SUBLANES:, :] = x.astype(BF16)
    xe = xe_ref[...]
    for c in range(d_ff // chunk):
        gcols = slice(c * chunk, (c + 1) * chunk)
        vcols = slice(d_ff + c * chunk, d_ff + (c + 1) * chunk)
        gate = _dot(xe, wup_ref[:, gcols])
        val = _dot(xe[SUBLANES:, :], wup_ref[:, vcols])
        conv = (cb_ref[:, gcols]
                + cw_ref[0:1, gcols] * pltpu.roll(gate, 2, axis=0)[SUBLANES:, :]
                + cw_ref[1:2, gcols] * pltpu.roll(gate, 1, axis=0)[SUBLANES:, :]
                + cw_ref[2:3, gcols] * gate[SUBLANES:, :])
        act = conv * (1.0 / (1.0 + jnp.exp(-conv)))
        h_ref[:, gcols] = (act * val).astype(BF16)
    f = _dot(h_ref[...], wdown_ref[...])
    r = DEEPNORM_ALPHA * x + f
    out_ref[...] = _layer_norm(r, g_ref[...], b_ref[...])


def _conv_ffn(x2d, w_up_bf16, conv_w, conv_b, w_down_bf16, ln_g, ln_b, seq, tm, chunk):
    tokens, d_model = x2d.shape
    d_ff = w_down_bf16.shape[0]
    seq_tiles = seq // tm
    halo_blocks = tm // SUBLANES
    fixed = lambda i: (0, 0)
    single = dict(pipeline_mode=pl.Buffered(1))
    return pl.pallas_call(
        functools.partial(_ffn_kernel, seq_tiles=seq_tiles, d_ff=d_ff, chunk=chunk),
        out_shape=jax.ShapeDtypeStruct((tokens, d_model), F32),
        grid=(tokens // tm,),
        in_specs=[
            pl.BlockSpec((tm, d_model), lambda i: (i, 0)),
            pl.BlockSpec((SUBLANES, d_model), lambda i: (jnp.maximum(i * halo_blocks - 1, 0), 0)),
            pl.BlockSpec((d_model, 2 * d_ff), fixed, **single),
            pl.BlockSpec((CONV_WIDTH, d_ff), fixed),
            pl.BlockSpec((1, d_ff), fixed),
            pl.BlockSpec((d_ff, d_model), fixed, **single),
            pl.BlockSpec((1, d_model), fixed),
            pl.BlockSpec((1, d_model), fixed),
        ],
        out_specs=pl.BlockSpec((tm, d_model), lambda i: (i, 0)),
        scratch_shapes=[pltpu.VMEM((tm + SUBLANES, d_model), BF16),
                        pltpu.VMEM((tm, d_ff), BF16)],
        compiler_params=_params("parallel"),
        name="conv_ffn_ln",
    )(x2d, x2d, w_up_bf16, conv_w, conv_b.reshape(1, -1), w_down_bf16,
      ln_g.reshape(1, -1), ln_b.reshape(1, -1))


def _ssm_in_kernel(x_ref, wt_ref, ut_ref):
    ut_ref[...] = _dot_nt(wt_ref[...], x_ref[...].astype(BF16)).astype(BF16)


def _ssm_in(x2d, w_in_t_bf16, tn):
    tokens, d_model = x2d.shape
    d_ssm = w_in_t_bf16.shape[0]
    return pl.pallas_call(
        _ssm_in_kernel,
        out_shape=jax.ShapeDtypeStruct((d_ssm, tokens), BF16),
        grid=(tokens // tn,),
        in_specs=[pl.BlockSpec((tn, d_model), lambda i: (i, 0)),
                  pl.BlockSpec((d_ssm, d_model), lambda i: (0, 0))],
        out_specs=pl.BlockSpec((d_ssm, tn), lambda i: (0, i)),
        compiler_params=_params("parallel"),
        name="ssm_in_proj",
    )(x2d, w_in_t_bf16)


def _ssm_core_kernel(ut_ref, kv_ref, pm_ref, cm_ref, la_ref, lb_ref, yt_ref, toep_ref, lhs_ref,
                     *, chunks_per_seq):
    gc = SSM_GROUP_CH
    L = SSM_CHUNK
    rows = ut_ref.shape[1]
    n2 = 2 * SSM_STATE

    for c in range(gc):
        lhs_ref[:, c * L:(c + 1) * L] = ut_ref[c]

    srow = lax.broadcasted_iota(jnp.int32, (L, L), 0)
    tcol = lax.broadcasted_iota(jnp.int32, (L, L), 1)
    causal = tcol >= srow

    def build(c, carry):
        for cp in range(gc):
            kv = kv_ref[c, pl.ds(cp, 1), :]
            blk = pltpu.roll(jnp.broadcast_to(kv, (L, L)), 0, 1, stride=1, stride_axis=0)
            blk = jnp.where(causal, blk, 0.0)
            toep_ref[pl.ds(pl.multiple_of(c * L, L), L), cp * L:(cp + 1) * L] = blk.astype(BF16)
        return carry

    lax.fori_loop(0, gc, build, 0)

    lhs = lhs_ref[...]
    y = _dot(lhs, toep_ref[...])
    st = _dot(lhs, pm_ref[...])

    a = la_ref[...]
    b = lb_ref[...]
    jrow = lax.broadcasted_iota(jnp.int32, (rows, n2), 0) % chunks_per_seq
    sh = 1
    while sh < chunks_per_seq:
        prev = jnp.where(jrow >= sh, pltpu.roll(st, sh, axis=0), 0.0)
        st = st + prev * a + pltpu.roll(prev, SSM_STATE, axis=1) * b
        a, b = a * a - b * b, 2.0 * a * b
        sh *= 2
    carried = jnp.where(jrow >= 1, pltpu.roll(st, 1, axis=0), 0.0)
    y = y + _dot(carried.astype(BF16), cm_ref[...])

    for cp in range(gc):
        yt_ref[cp] = y[:, cp * L:(cp + 1) * L]


def _ssm_core(ut, kvec, pmat, cmat, lam_a, lam_b, chunks_per_seq):
    d_ssm, tokens = ut.shape
    gc, L, n2 = SSM_GROUP_CH, SSM_CHUNK, 2 * SSM_STATE
    groups = d_ssm // gc
    rows = tokens // L
    ut3 = ut.reshape(d_ssm, rows, L)
    yt = pl.pallas_call(
        functools.partial(_ssm_core_kernel, chunks_per_seq=chunks_per_seq),
        out_shape=jax.ShapeDtypeStruct((d_ssm, rows, L), F32),
        grid=(groups,),
        in_specs=[
            pl.BlockSpec((gc, rows, L), lambda g: (g, 0, 0)),
            pl.BlockSpec((None, gc, gc, L), lambda g: (g, 0, 0, 0)),
            pl.BlockSpec((None, gc * L, n2), lambda g: (g, 0, 0)),
            pl.BlockSpec((None, n2, gc * L), lambda g: (g, 0, 0)),
            pl.BlockSpec((None, 1, n2), lambda g: (g, 0, 0)),
            pl.BlockSpec((None, 1, n2), lambda g: (g, 0, 0)),
        ],
        out_specs=pl.BlockSpec((gc, rows, L), lambda g: (g, 0, 0)),
        scratch_shapes=[pltpu.VMEM((gc * L, gc * L), BF16),
                        pltpu.VMEM((rows, gc * L), BF16)],
        compiler_params=_params("parallel"),
        name="ssm_chunk_conv",
    )(ut3, kvec, pmat, cmat, lam_a, lam_b)
    return yt.reshape(d_ssm, tokens)


def _ssm_operators(a_re, a_im, log_dt, b_re, b_im, c_re, c_im):
    L = SSM_CHUNK
    hi = lax.Precision.HIGHEST
    a_re = a_re.astype(F32)
    a_im = a_im.astype(F32)
    dt = jnp.exp(log_dt.astype(F32))[:, None]
    mag = jnp.exp(a_re * dt)
    lam_re = mag * jnp.cos(a_im * dt)
    lam_im = mag * jnp.sin(a_im * dt)
    nr, ni = lam_re - 1.0, lam_im
    den = a_re * a_re + a_im * a_im
    coef_re = ((nr * a_re + ni * a_im) / den)[..., None]
    coef_im = ((ni * a_re - nr * a_im) / den)[..., None]
    b_re = b_re.astype(F32)
    b_im = b_im.astype(F32)
    bb_re = coef_re * b_re - coef_im * b_im
    bb_im = coef_re * b_im + coef_im * b_re
    cr = jnp.swapaxes(c_re.astype(F32), 1, 2)
    ci = jnp.swapaxes(c_im.astype(F32), 1, 2)

    k = jnp.arange(L + 1, dtype=F32)
    pmag = jnp.exp((a_re * dt)[..., None] * k)
    ang = (a_im * dt)[..., None] * k
    pw_re = pmag * jnp.cos(ang)
    pw_im = pmag * jnp.sin(ang)

    cb_re = cr[:, :, None, :] * bb_re[..., None] - ci[:, :, None, :] * bb_im[..., None]
    cb_im = cr[:, :, None, :] * bb_im[..., None] + ci[:, :, None, :] * bb_re[..., None]
    kvec = (jnp.einsum('gncd,gnk->gcdk', cb_re, pw_re[..., :L], precision=hi)
            - jnp.einsum('gncd,gnk->gcdk', cb_im, pw_im[..., :L], precision=hi))

    rp_re = jnp.flip(pw_re[..., :L], axis=-1)
    rp_im = jnp.flip(pw_im[..., :L], axis=-1)
    pm_re = rp_re[:, :, None, :] * bb_re[..., None] - rp_im[:, :, None, :] * bb_im[..., None]
    pm_im = rp_re[:, :, None, :] * bb_im[..., None] + rp_im[:, :, None, :] * bb_re[..., None]
    pmat = jnp.concatenate([pm_re, pm_im], axis=1)
    g_, n2, c_, _ = pmat.shape
    pmat = jnp.transpose(pmat, (0, 2, 3, 1)).reshape(g_, c_ * L, n2).astype(BF16)

    fp_re = pw_re[..., 1:]
    fp_im = pw_im[..., 1:]
    cm_re = cr[..., None] * fp_re[:, :, None, :] - ci[..., None] * fp_im[:, :, None, :]
    cm_im = -(cr[..., None] * fp_im[:, :, None, :] + ci[..., None] * fp_re[:, :, None, :])
    cmat = jnp.concatenate([cm_re, cm_im], axis=1).reshape(g_, n2, c_ * L).astype(BF16)

    lam_a = jnp.concatenate([pw_re[..., L], pw_re[..., L]], axis=-1)[:, None, :]
    lam_b = jnp.concatenate([-pw_im[..., L], pw_im[..., L]], axis=-1)[:, None, :]
    return kvec, pmat, cmat, lam_a, lam_b


def _ssm_out_kernel(yt_ref, ut_ref, d_ref, wg_ref, bg_ref, wo_ref, x_ref, g_ref, b_ref, out_ref):
    y = yt_ref[...] + d_ref[...] * ut_ref[...].astype(F32)
    z = 0.5 * y * (1.0 + jnp.tanh(math.sqrt(2.0 / math.pi) * (y + 0.044715 * (y * y * y))))
    gate = _dot(wg_ref[...], z.astype(BF16)) + bg_ref[...]
    zs = z * (1.0 / (1.0 + jnp.exp(-gate)))
    f = _dot_tn(zs.astype(BF16), wo_ref[...])
    r = DEEPNORM_ALPHA * x_ref[...] + f
    out_ref[...] = _layer_norm(r, g_ref[...], b_ref[...])


def _ssm_out(yt, ut, d_skip, w_glu_t_bf16, b_glu, w_out_bf16, x2d, ln_g, ln_b, tn):
    tokens, d_model = x2d.shape
    d_ssm = yt.shape[0]
    fixed = lambda i: (0, 0)
    return pl.pallas_call(
        _ssm_out_kernel,
        out_shape=jax.ShapeDtypeStruct((tokens, d_model), F32),
        grid=(tokens // tn,),
        in_specs=[
            pl.BlockSpec((d_ssm, tn), lambda i: (0, i)),
            pl.BlockSpec((d_ssm, tn), lambda i: (0, i)),
            pl.BlockSpec((d_ssm, 1), fixed),
            pl.BlockSpec((d_ssm, d_ssm), fixed),
            pl.BlockSpec((d_ssm, 1), fixed),
            pl.BlockSpec((d_ssm, d_model), fixed),
            pl.BlockSpec((tn, d_model), lambda i: (i, 0)),
            pl.BlockSpec((1, d_model), fixed),
            pl.BlockSpec((1, d_model), fixed),
        ],
        out_specs=pl.BlockSpec((tn, d_model), lambda i: (i, 0)),
        compiler_params=_params("parallel"),
        name="ssm_glu_out_ln",
    )(yt, ut, d_skip.reshape(-1, 1), w_glu_t_bf16, b_glu.reshape(-1, 1), w_out_bf16, x2d,
      ln_g.reshape(1, -1), ln_b.reshape(1, -1))


def _rope_tables(seq):
    pos = jnp.arange(seq, dtype=F32)
    inv_freq = ROPE_THETA ** (-jnp.arange(0, HEAD_DIM, 2, dtype=F32) / HEAD_DIM)
    ang = pos[:, None] * inv_freq[None, :]
    cos = jnp.cos(ang)
    sin = jnp.sin(ang)
    return jnp.concatenate([cos, cos], axis=1), jnp.concatenate([-sin, sin], axis=1)


def _row_tile(seq, want):
    tm = min(want, seq)
    assert seq % tm == 0
    return tm


def kernel(x, attn_w_in, attn_w_out, ssm_w_in, ssm_a_re, ssm_a_im, ssm_log_dt, ssm_b_re, ssm_b_im,
           ssm_c_re, ssm_c_im, ssm_d, ssm_w_glu, ssm_b_glu, ssm_w_out, ffn_w_up, ffn_conv_w,
           ffn_conv_b, ffn_w_down, ln_g, ln_b):
    batch, seq, d_model = x.shape
    assert seq % DILATION_PAIRS[-1][0] == 0 and seq % SSM_CHUNK == 0
    assert all(w // d == ATTN_BLOCK for w, d in DILATION_PAIRS)
    h = x.reshape(batch * seq, d_model)
    ffn_chunk = 256

    def ffn(h, i):
        return _conv_ffn(h, ffn_w_up[i].astype(BF16), ffn_conv_w[i], ffn_conv_b[i],
                         ffn_w_down[i].astype(BF16), ln_g[i, 1], ln_b[i, 1], seq,
                         _row_tile(seq, 1024), ffn_chunk)

    cos_tab, sin_tab = _rope_tables(seq)
    qkv = _qkv_rope(h, attn_w_in[0].astype(BF16), cos_tab, sin_tab, seq, _row_tile(seq, 1024))
    os, lses = [], []
    for g, (_, dilation) in enumerate(DILATION_PAIRS):
        o, lse = _dilated_attention_group(qkv, g, dilation, batch, seq)
        os.append(o)
        lses.append(lse)
    h = _merge_out(os, lses, h, attn_w_out[0].astype(BF16), ln_g[0, 0], ln_b[0, 0],
                   _row_tile(seq, 512))
    h = ffn(h, 0)

    kvec, pmat, cmat, lam_a, lam_b = _ssm_operators(
        ssm_a_re[0], ssm_a_im[0], ssm_log_dt[0], ssm_b_re[0], ssm_b_im[0], ssm_c_re[0], ssm_c_im[0])
    ut = _ssm_in(h, ssm_w_in[0].T.astype(BF16), _row_tile(seq, 1024))
    yt = _ssm_core(ut, kvec, pmat, cmat, lam_a, lam_b, seq // SSM_CHUNK)
    h = _ssm_out(yt, ut, ssm_d[0], ssm_w_glu[0].T.astype(BF16), ssm_b_glu[0],
                 ssm_w_out[0].astype(BF16), h, ln_g[1, 0], ln_b[1, 0], _row_tile(seq, 512))
    h = ffn(h, 1)
    return h.reshape(batch, seq, d_model)
```

```python
import functools
import math

import jax
import jax.numpy as jnp
from jax import lax
from jax.experimental import pallas as pl
from jax.experimental.pallas import tpu as pltpu

F32 = jnp.float32
BF16 = jnp.bfloat16

DEPTH = 2
DILATION_PAIRS = ((128, 1), (512, 4), (2048, 16))
N_GROUPS = len(DILATION_PAIRS)
HEADS = 8
HEAD_DIM = 128
ROPE_THETA = 10000.0
SSM_GROUP_CH = 16
SSM_STATE = 64
CONV_WIDTH = 3
DEEPNORM_ALPHA = (2.0 * DEPTH) ** 0.25
LN_EPS = 1e-5
NEG_INF = -1e30
LOG2E = math.log2(math.e)
LN2 = math.log(2.0)

LANES = 128
SUBLANES = 8
VMEM_LIMIT_BYTES = 56 * 1024 * 1024

ATTN_BLOCK = 128
SSM_CHUNK = 128


def _params(*semantics):
    return pltpu.CompilerParams(dimension_semantics=semantics,
                                vmem_limit_bytes=VMEM_LIMIT_BYTES)


def _layer_norm(r, g, b):
    mu = jnp.mean(r, axis=-1, keepdims=True)
    d = r - mu
    var = jnp.mean(d * d, axis=-1, keepdims=True)
    return d * lax.rsqrt(var + LN_EPS) * g + b


def _dot(a, b):
    return jnp.dot(a, b, preferred_element_type=F32)


def _dot_nt(a, b):
    return lax.dot_general(a, b, (((1,), (1,)), ((), ())), preferred_element_type=F32)


def _dot_tn(a, b):
    return lax.dot_general(a, b, (((0,), (0,)), ((), ())), preferred_element_type=F32)


def _residue_rows(r, n, d):
    return pl.ds(r, n, stride=d) if d > 1 else slice(None)


def _qkv_kernel(*refs, d):
    n_slabs = len(refs) - 7
    x_refs = refs[:n_slabs]
    w_ref, cos_ref, sin_ref, o_ref, xb_ref, cs_ref, sn_ref = refs[n_slabs:]
    kind = pl.program_id(1)
    tm = cos_ref.shape[0]
    n = tm // d

    @pl.when(kind == 0)
    def _():
        for r in range(d):
            rows = _residue_rows(r, n, d)
            for c, x_ref in enumerate(x_refs):
                xb_ref[r * n:(r + 1) * n, c * LANES:(c + 1) * LANES] = x_ref[rows, :].astype(BF16)
            cs_ref[r * n:(r + 1) * n, :] = cos_ref[rows, :]
            sn_ref[r * n:(r + 1) * n, :] = sin_ref[rows, :]

    acc = _dot(xb_ref[...], w_ref[...])

    @pl.when(kind == 2)
    def _():
        for r in range(d):
            o_ref[r] = acc[r * n:(r + 1) * n, :].astype(BF16)

    @pl.when(kind != 2)
    def _():
        scale = jnp.where(kind == 0, HEAD_DIM ** -0.5 * LOG2E, 1.0).astype(F32)
        cos = cs_ref[...]
        sin = sn_ref[...]
        for h in range(HEADS):
            t = acc[:, h * HEAD_DIM:(h + 1) * HEAD_DIM]
            rot = ((t * cos + pltpu.roll(t, HEAD_DIM // 2, axis=1) * sin) * scale).astype(BF16)
            for r in range(d):
                o_ref[r, :, h * HEAD_DIM:(h + 1) * HEAD_DIM] = rot[r * n:(r + 1) * n, :]


def _qkv_rope(x2d, w_in_bf16, cos_tab, sin_tab, g, d, batch, seq, tm):
    d_model = x2d.shape[1]
    hw = HEADS * HEAD_DIM
    seq_tiles = seq // tm
    n = tm // d
    return pl.pallas_call(
        functools.partial(_qkv_kernel, d=d),
        out_shape=jax.ShapeDtypeStruct((batch, d, seq // d, 3 * hw), BF16),
        grid=(batch * seq_tiles, 3),
        in_specs=[pl.BlockSpec((tm, LANES), lambda i, j, c=c: (i, c))
                  for c in range(d_model // LANES)] + [
            pl.BlockSpec((d_model, hw), lambda i, j: (0, 3 * g + j)),
            pl.BlockSpec((tm, HEAD_DIM), lambda i, j: (i % seq_tiles, 0)),
            pl.BlockSpec((tm, HEAD_DIM), lambda i, j: (i % seq_tiles, 0)),
        ],
        out_specs=pl.BlockSpec((None, d, n, hw),
                               lambda i, j: (i // seq_tiles, 0, i % seq_tiles, j)),
        scratch_shapes=[pltpu.VMEM((tm, d_model), BF16),
                        pltpu.VMEM((tm, HEAD_DIM), F32),
                        pltpu.VMEM((tm, HEAD_DIM), F32)],
        compiler_params=_params("parallel", "arbitrary"),
        name=f"qkv_rope_g{g}",
    )(*([x2d] * (d_model // LANES)), w_in_bf16, cos_tab, sin_tab)


def _attn_kernel(q_ref, k_ref, v_ref, kp_ref, vp_ref, o_ref, lse_ref, *, nblk):
    m = pl.program_id(2)
    blk_rows = ATTN_BLOCK
    row = lax.broadcasted_iota(jnp.int32, (blk_rows, 2 * blk_rows), 0)
    col = lax.broadcasted_iota(jnp.int32, (blk_rows, 2 * blk_rows), 1)
    band = jnp.logical_and(col >= row, col <= row + blk_rows)
    first_band = jnp.logical_and(band, jnp.logical_or(col >= blk_rows, m > 0))
    lane = lax.broadcasted_iota(jnp.int32, (blk_rows, LANES), 1)
    ones = jnp.ones((2 * blk_rows, HEAD_DIM), BF16)

    def block(q_rows, keys, vals, mask):
        lse_tile = jnp.zeros((blk_rows, LANES), F32)
        for h in range(HEADS):
            cols = slice(h * HEAD_DIM, (h + 1) * HEAD_DIM)
            s = jnp.where(mask, _dot_nt(q_ref[q_rows, cols], keys(cols)), NEG_INF)
            mx = jnp.max(jnp.maximum(s[:, :blk_rows], s[:, blk_rows:]), axis=1, keepdims=True)
            p = jnp.exp2(s - mx).astype(BF16)
            oe = _dot(p, jnp.concatenate([vals(cols), ones], axis=1))
            l = oe[:, HEAD_DIM:]
            o_ref[q_rows, cols] = (oe[:, :HEAD_DIM] / l).astype(BF16)
            lse_tile = jnp.where(lane == h, mx * LN2 + jnp.log(l), lse_tile)
        lse_ref[q_rows, :] = lse_tile

    head = slice(0, blk_rows)
    block(head,
          lambda cols: jnp.concatenate([kp_ref[:, cols], k_ref[head, cols]], axis=0),
          lambda cols: jnp.concatenate([vp_ref[:, cols], v_ref[head, cols]], axis=0),
          first_band)

    def body(blk, carry):
        q_rows = pl.ds(pl.multiple_of(blk * blk_rows, blk_rows), blk_rows)
        kv_rows = pl.ds(pl.multiple_of((blk - 1) * blk_rows, blk_rows), 2 * blk_rows)
        block(q_rows, lambda cols: k_ref[kv_rows, cols], lambda cols: v_ref[kv_rows, cols], band)
        return carry

    lax.fori_loop(1, nblk, body, 0)


def _dilated_attention_group(qkv_g, g, rows):
    batch, d, sub_len, _ = qkv_g.shape
    hw = HEADS * HEAD_DIM
    rows = min(rows, sub_len)
    nblk = rows // ATTN_BLOCK

    def main_spec(kind):
        return pl.BlockSpec((None, None, rows, hw), lambda b, r, m: (b, r, m, kind))

    def prev_spec(kind):
        return pl.BlockSpec((None, None, ATTN_BLOCK, hw),
                            lambda b, r, m: (b, r, jnp.maximum(m * nblk - 1, 0), kind))

    return pl.pallas_call(
        functools.partial(_attn_kernel, nblk=nblk),
        out_shape=(jax.ShapeDtypeStruct((batch, d, sub_len, hw), BF16),
                   jax.ShapeDtypeStruct((batch, d, sub_len, LANES), F32)),
        grid=(batch, d, sub_len // rows),
        in_specs=[main_spec(0), main_spec(1), main_spec(2), prev_spec(1), prev_spec(2)],
        out_specs=(pl.BlockSpec((None, None, rows, hw), lambda b, r, m: (b, r, m, 0)),
                   pl.BlockSpec((None, None, rows, LANES), lambda b, r, m: (b, r, m, 0))),
        compiler_params=_params("parallel", "parallel", "arbitrary"),
        name=f"dilated_attn_g{g}",
    )(qkv_g, qkv_g, qkv_g, qkv_g, qkv_g)


def _merge_out_kernel(o0_ref, o1_ref, o2_ref, l0_ref, l1_ref, l2_ref, e_ref, x_ref, w_ref,
                      g_ref, b_ref, out_ref, on_ref, ln_ref):
    tm = x_ref.shape[0]
    outs, lses = [], []
    for gi, (o_ref, l_ref) in enumerate(((o0_ref, l0_ref), (o1_ref, l1_ref), (o2_ref, l2_ref))):
        d = o_ref.shape[0]
        n = tm // d
        for r in range(d):
            rows = _residue_rows(r, n, d)
            o_r = o_ref[r].astype(F32)
            for c in range(on_ref.shape[1]):
                on_ref[gi, c, rows, :] = o_r[:, c * LANES:(c + 1) * LANES]
            ln_ref[gi, rows, :] = l_ref[r]
        outs.append(jnp.concatenate([on_ref[gi, c] for c in range(on_ref.shape[1])], axis=1))
        lses.append(ln_ref[gi])
    mx = jnp.maximum(jnp.maximum(lses[0], lses[1]), lses[2])
    ws = [jnp.exp(l - mx) for l in lses]
    inv = 1.0 / (ws[0] + ws[1] + ws[2])
    e = e_ref[...]
    merged = None
    for w, o in zip(ws, outs):
        w = w * inv
        w_hi = w.astype(BF16)
        w_lo = (w - w_hi.astype(F32)).astype(BF16)
        w_full = _dot(w_hi, e) + _dot(w_lo, e)
        term = w_full * o
        merged = term if merged is None else merged + term
    y = _dot(merged.astype(BF16), w_ref[...])
    r = DEEPNORM_ALPHA * x_ref[...] + y
    out_ref[...] = _layer_norm(r, g_ref[...], b_ref[...])


def _merge_out(os, lses, x2d, w_out_bf16, ln_g, ln_b, seq, tm):
    tokens, d_model = x2d.shape
    hw = HEADS * HEAD_DIM
    seq_tiles = seq // tm
    expand = (jnp.arange(LANES)[:, None] == (jnp.arange(hw)[None, :] // HEAD_DIM)).astype(BF16)
    row = lambda i: (i, 0)
    fixed = lambda i: (0, 0)

    def group_spec(arr):
        d, width = arr.shape[1], arr.shape[3]
        return pl.BlockSpec((None, d, tm // d, width),
                            lambda i: (i // seq_tiles, 0, i % seq_tiles, 0))

    return pl.pallas_call(
        _merge_out_kernel,
        out_shape=jax.ShapeDtypeStruct((tokens, d_model), F32),
        grid=(tokens // tm,),
        in_specs=[group_spec(a) for a in os] + [group_spec(a) for a in lses] + [
            pl.BlockSpec((LANES, hw), fixed),
            pl.BlockSpec((tm, d_model), row),
            pl.BlockSpec((hw, d_model), fixed),
            pl.BlockSpec((1, d_model), fixed),
            pl.BlockSpec((1, d_model), fixed),
        ],
        out_specs=pl.BlockSpec((tm, d_model), row),
        scratch_shapes=[pltpu.VMEM((N_GROUPS, hw // LANES, tm, LANES), F32),
                        pltpu.VMEM((N_GROUPS, tm, LANES), F32)],
        compiler_params=_params("parallel"),
        name="attn_merge_out_ln",
    )(*os, *lses, expand, x2d, w_out_bf16, ln_g.reshape(1, -1), ln_b.reshape(1, -1))


def _ffn_kernel(x_ref, halo_ref, wup_ref, cw_ref, cb_ref, wdown_ref, g_ref, b_ref, out_ref,
                xe_ref, h_ref, *, seq_tiles, d_ff, chunk):
    i = pl.program_id(0)
    x = x_ref[...]
    first = (i % seq_tiles) == 0
    xe_ref[0:SUBLANES, :] = jnp.where(first, 0.0, halo_ref[...]).astype(BF16)
    xe_ref[SUBLANES:, :] = x.astype(BF16)
    xe = xe_ref[...]
    for c in range(d_ff // chunk):
        gcols = slice(c * chunk, (c + 1) * chunk)
        vcols = slice(d_ff + c * chunk, d_ff + (c + 1) * chunk)
        gate = _dot(xe, wup_ref[:, gcols])
        val = _dot(xe[SUBLANES:, :], wup_ref[:, vcols])
        conv = (cb_ref[:, gcols]
                + cw_ref[0:1, gcols] * pltpu.roll(gate, 2, axis=0)[SUBLANES:, :]
                + cw_ref[1:2, gcols] * pltpu.roll(gate, 1, axis=0)[SUBLANES:, :]
                + cw_ref[2:3, gcols] * gate[SUBLANES:, :])
        act = conv * (1.0 / (1.0 + jnp.exp(-conv)))
        h_ref[:, gcols] = (act * val).astype(BF16)
    f = _dot(h_ref[...], wdown_ref[...])
    r = DEEPNORM_ALPHA * x + f
    out_ref[...] = _layer_norm(r, g_ref[...], b_ref[...])


def _conv_ffn(x2d, w_up_bf16, conv_w, conv_b, w_down_bf16, ln_g, ln_b, seq, tm, chunk):
    tokens, d_model = x2d.shape
    d_ff = w_down_bf16.shape[0]
    seq_tiles = seq // tm
    halo_blocks = tm // SUBLANES
    fixed = lambda i: (0, 0)
    single = dict(pipeline_mode=pl.Buffered(1))
    return pl.pallas_call(
        functools.partial(_ffn_kernel, seq_tiles=seq_tiles, d_ff=d_ff, chunk=chunk),
        out_shape=jax.ShapeDtypeStruct((tokens, d_model), F32),
        grid=(tokens // tm,),
        in_specs=[
            pl.BlockSpec((tm, d_model), lambda i: (i, 0)),
            pl.BlockSpec((SUBLANES, d_model), lambda i: (jnp.maximum(i * halo_blocks - 1, 0), 0)),
            pl.BlockSpec((d_model, 2 * d_ff), fixed, **single),
            pl.BlockSpec((CONV_WIDTH, d_ff), fixed),
            pl.BlockSpec((1, d_ff), fixed),
            pl.BlockSpec((d_ff, d_model), fixed, **single),
            pl.BlockSpec((1, d_model), fixed),
            pl.BlockSpec((1, d_model), fixed),
        ],
        out_specs=pl.BlockSpec((tm, d_model), lambda i: (i, 0)),
        scratch_shapes=[pltpu.VMEM((tm + SUBLANES, d_model), BF16),
                        pltpu.VMEM((tm, d_ff), BF16)],
        compiler_params=_params("parallel"),
        name="conv_ffn_ln",
    )(x2d, x2d, w_up_bf16, conv_w, conv_b.reshape(1, -1), w_down_bf16,
      ln_g.reshape(1, -1), ln_b.reshape(1, -1))


def _ssm_in_kernel(x_ref, wt_ref, ut_ref):
    u = _dot_nt(wt_ref[...], x_ref[...].astype(BF16))
    for q in range(ut_ref.shape[0]):
        ut_ref[q] = u[:, q * SSM_CHUNK:(q + 1) * SSM_CHUNK]


def _ssm_in(x2d, w_in_t_bf16, tn):
    tokens, d_model = x2d.shape
    d_ssm = w_in_t_bf16.shape[0]
    return pl.pallas_call(
        _ssm_in_kernel,
        out_shape=jax.ShapeDtypeStruct((tokens // SSM_CHUNK, d_ssm, SSM_CHUNK), F32),
        grid=(tokens // tn,),
        in_specs=[pl.BlockSpec((tn, d_model), lambda i: (i, 0)),
                  pl.BlockSpec((d_ssm, d_model), lambda i: (0, 0))],
        out_specs=pl.BlockSpec((tn // SSM_CHUNK, d_ssm, SSM_CHUNK), lambda i: (i, 0, 0)),
        compiler_params=_params("parallel"),
        name="ssm_in_proj",
    )(x2d, w_in_t_bf16)


def _ssm_core_kernel(ut_ref, kv_ref, pm_ref, cm_ref, la_ref, lb_ref, yt_ref, toep_ref, lhs_ref,
                     *, chunks_per_seq):
    gc = SSM_GROUP_CH
    L = SSM_CHUNK
    rows = ut_ref.shape[0]
    n2 = 2 * SSM_STATE

    for c in range(gc):
        lhs_ref[:, c * L:(c + 1) * L] = ut_ref[:, c, :].astype(BF16)

    srow = lax.broadcasted_iota(jnp.int32, (L, L), 0)
    tcol = lax.broadcasted_iota(jnp.int32, (L, L), 1)
    causal = tcol >= srow

    def build(c, carry):
        for cp in range(gc):
            kv = kv_ref[c, pl.ds(cp, 1), :]
            blk = pltpu.roll(jnp.broadcast_to(kv, (L, L)), 0, 1, stride=1, stride_axis=0)
            blk = jnp.where(causal, blk, 0.0)
            toep_ref[pl.ds(pl.multiple_of(c * L, L), L), cp * L:(cp + 1) * L] = blk.astype(BF16)
        return carry

    lax.fori_loop(0, gc, build, 0)

    lhs = lhs_ref[...]
    y = _dot(lhs, toep_ref[...])
    st = _dot(lhs, pm_ref[...])

    a = la_ref[...]
    b = lb_ref[...]
    jrow = lax.broadcasted_iota(jnp.int32, (rows, n2), 0) % chunks_per_seq
    sh = 1
    while sh < chunks_per_seq:
        prev = jnp.where(jrow >= sh, pltpu.roll(st, sh, axis=0), 0.0)
        st = st + prev * a + pltpu.roll(prev, SSM_STATE, axis=1) * b
        a, b = a * a - b * b, 2.0 * a * b
        sh *= 2
    carried = jnp.where(jrow >= 1, pltpu.roll(st, 1, axis=0), 0.0)
    y = y + _dot(carried.astype(BF16), cm_ref[...])

    for cp in range(gc):
        yt_ref[:, cp, :] = y[:, cp * L:(cp + 1) * L]


def _ssm_core(ut, kvec, pmat, cmat, lam_a, lam_b, chunks_per_seq):
    rows, d_ssm, L = ut.shape
    gc, n2 = SSM_GROUP_CH, 2 * SSM_STATE
    return pl.pallas_call(
        functools.partial(_ssm_core_kernel, chunks_per_seq=chunks_per_seq),
        out_shape=jax.ShapeDtypeStruct((rows, d_ssm, L), F32),
        grid=(d_ssm // gc,),
        in_specs=[
            pl.BlockSpec((rows, gc, L), lambda g: (0, g, 0)),
            pl.BlockSpec((None, gc, gc, L), lambda g: (g, 0, 0, 0)),
            pl.BlockSpec((None, gc * L, n2), lambda g: (g, 0, 0)),
            pl.BlockSpec((None, n2, gc * L), lambda g: (g, 0, 0)),
            pl.BlockSpec((None, 1, n2), lambda g: (g, 0, 0)),
            pl.BlockSpec((None, 1, n2), lambda g: (g, 0, 0)),
        ],
        out_specs=pl.BlockSpec((rows, gc, L), lambda g: (0, g, 0)),
        scratch_shapes=[pltpu.VMEM((gc * L, gc * L), BF16),
                        pltpu.VMEM((rows, gc * L), BF16)],
        compiler_params=_params("parallel"),
        name="ssm_chunk_conv",
    )(ut, kvec, pmat, cmat, lam_a, lam_b)


def _ssm_operators(a_re, a_im, log_dt, b_re, b_im, c_re, c_im):
    L = SSM_CHUNK
    hi = lax.Precision.HIGHEST
    a_re = a_re.astype(F32)
    a_im = a_im.astype(F32)
    dt = jnp.exp(log_dt.astype(F32))[:, None]
    mag = jnp.exp(a_re * dt)
    lam_re = mag * jnp.cos(a_im * dt)
    lam_im = mag * jnp.sin(a_im * dt)
    nr, ni = lam_re - 1.0, lam_im
    den = a_re * a_re + a_im * a_im
    coef_re = ((nr * a_re + ni * a_im) / den)[..., None]
    coef_im = ((ni * a_re - nr * a_im) / den)[..., None]
    b_re = b_re.astype(F32)
    b_im = b_im.astype(F32)
    bb_re = coef_re * b_re - coef_im * b_im
    bb_im = coef_re * b_im + coef_im * b_re
    cr = jnp.swapaxes(c_re.astype(F32), 1, 2)
    ci = jnp.swapaxes(c_im.astype(F32), 1, 2)

    k = jnp.arange(L + 1, dtype=F32)
    pmag = jnp.exp((a_re * dt)[..., None] * k)
    ang = (a_im * dt)[..., None] * k
    pw_re = pmag * jnp.cos(ang)
    pw_im = pmag * jnp.sin(ang)

    cb_re = cr[:, :, None, :] * bb_re[..., None] - ci[:, :, None, :] * bb_im[..., None]
    cb_im = cr[:, :, None, :] * bb_im[..., None] + ci[:, :, None, :] * bb_re[..., None]
    kvec = (jnp.einsum('gncd,gnk->gcdk', cb_re, pw_re[..., :L], precision=hi)
            - jnp.einsum('gncd,gnk->gcdk', cb_im, pw_im[..., :L], precision=hi))

    rp_re = jnp.flip(pw_re[..., :L], axis=-1)
    rp_im = jnp.flip(pw_im[..., :L], axis=-1)
    pm_re = rp_re[:, :, None, :] * bb_re[..., None] - rp_im[:, :, None, :] * bb_im[..., None]
    pm_im = rp_re[:, :, None, :] * bb_im[..., None] + rp_im[:, :, None, :] * bb_re[..., None]
    pmat = jnp.concatenate([pm_re, pm_im], axis=1)
    g_, n2, c_, _ = pmat.shape
    pmat = jnp.transpose(pmat, (0, 2, 3, 1)).reshape(g_, c_ * L, n2).astype(BF16)

    fp_re = pw_re[..., 1:]
    fp_im = pw_im[..., 1:]
    cm_re = cr[..., None] * fp_re[:, :, None, :] - ci[..., None] * fp_im[:, :, None, :]
    cm_im = -(cr[..., None] * fp_im[:, :, None, :] + ci[..., None] * fp_re[:, :, None, :])
    cmat = jnp.concatenate([cm_re, cm_im], axis=1).reshape(g_, n2, c_ * L).astype(BF16)

    lam_a = jnp.concatenate([pw_re[..., L], pw_re[..., L]], axis=-1)[:, None, :]
    lam_b = jnp.concatenate([-pw_im[..., L], pw_im[..., L]], axis=-1)[:, None, :]
    return kvec, pmat, cmat, lam_a, lam_b


def _ssm_out_kernel(yt_ref, ut_ref, d_ref, wg_ref, bg_ref, wo_ref, x_ref, g_ref, b_ref, out_ref):
    nq = yt_ref.shape[0]
    yt = jnp.concatenate([yt_ref[q] for q in range(nq)], axis=1)
    ut = jnp.concatenate([ut_ref[q] for q in range(nq)], axis=1)
    y = yt + d_ref[...] * ut
    z = 0.5 * y * (1.0 + jnp.tanh(math.sqrt(2.0 / math.pi) * (y + 0.044715 * (y * y * y))))
    gate = _dot(wg_ref[...], z.astype(BF16)) + bg_ref[...]
    zs = z * (1.0 / (1.0 + jnp.exp(-gate)))
    f = _dot_tn(zs.astype(BF16), wo_ref[...])
    r = DEEPNORM_ALPHA * x_ref[...] + f
    out_ref[...] = _layer_norm(r, g_ref[...], b_ref[...])


def _ssm_out(yt, ut, d_skip, w_glu_t_bf16, b_glu, w_out_bf16, x2d, ln_g, ln_b, tn):
    tokens, d_model = x2d.shape
    d_ssm = yt.shape[1]
    nq = tn // SSM_CHUNK
    fixed = lambda i: (0, 0)
    chunked = pl.BlockSpec((nq, d_ssm, SSM_CHUNK), lambda i: (i, 0, 0))
    return pl.pallas_call(
        _ssm_out_kernel,
        out_shape=jax.ShapeDtypeStruct((tokens, d_model), F32),
        grid=(tokens // tn,),
        in_specs=[
            chunked,
            chunked,
            pl.BlockSpec((d_ssm, 1), fixed),
            pl.BlockSpec((d_ssm, d_ssm), fixed),
            pl.BlockSpec((d_ssm, 1), fixed),
            pl.BlockSpec((d_ssm, d_model), fixed),
            pl.BlockSpec((tn, d_model), lambda i: (i, 0)),
            pl.BlockSpec((1, d_model), fixed),
            pl.BlockSpec((1, d_model), fixed),
        ],
        out_specs=pl.BlockSpec((tn, d_model), lambda i: (i, 0)),
        compiler_params=_params("parallel"),
        name="ssm_glu_out_ln",
    )(yt, ut, d_skip.reshape(-1, 1), w_glu_t_bf16, b_glu.reshape(-1, 1), w_out_bf16, x2d,
      ln_g.reshape(1, -1), ln_b.reshape(1, -1))


def _rope_tables(seq):
    pos = jnp.arange(seq, dtype=F32)
    inv_freq = ROPE_THETA ** (-jnp.arange(0, HEAD_DIM, 2, dtype=F32) / HEAD_DIM)
    ang = pos[:, None] * inv_freq[None, :]
    cos = jnp.cos(ang)
    sin = jnp.sin(ang)
    return jnp.concatenate([cos, cos], axis=1), jnp.concatenate([-sin, sin], axis=1)


def _row_tile(seq, want):
    tm = min(want, seq)
    assert seq % tm == 0
    return tm


def kernel(x, attn_w_in, attn_w_out, ssm_w_in, ssm_a_re, ssm_a_im, ssm_log_dt, ssm_b_re, ssm_b_im,
           ssm_c_re, ssm_c_im, ssm_d, ssm_w_glu, ssm_b_glu, ssm_w_out, ffn_w_up, ffn_conv_w,
           ffn_conv_b, ffn_w_down, ln_g, ln_b):
    batch, seq, d_model = x.shape
    assert seq % DILATION_PAIRS[-1][0] == 0 and seq % SSM_CHUNK == 0
    assert all(w // d == ATTN_BLOCK for w, d in DILATION_PAIRS)
    h = x.reshape(batch * seq, d_model)
    ffn_chunk = 256

    def ffn(h, i):
        return _conv_ffn(h, ffn_w_up[i].astype(BF16), ffn_conv_w[i], ffn_conv_b[i],
                         ffn_w_down[i].astype(BF16), ln_g[i, 1], ln_b[i, 1], seq,
                         _row_tile(seq, 1024), ffn_chunk)

    cos_tab, sin_tab = _rope_tables(seq)
    w_in = attn_w_in[0].astype(BF16)
    os, lses = [], []
    for g, (_, dilation) in enumerate(DILATION_PAIRS):
        qkv_g = _qkv_rope(h, w_in, cos_tab, sin_tab, g, dilation, batch, seq, _row_tile(seq, 1024))
        o, lse = _dilated_attention_group(qkv_g, g, 1024)
        os.append(o)
        lses.append(lse)
    h = _merge_out(os, lses, h, attn_w_out[0].astype(BF16), ln_g[0, 0], ln_b[0, 0], seq,
                   _row_tile(seq, 512))
    h = ffn(h, 0)

    kvec, pmat, cmat, lam_a, lam_b = _ssm_operators(
        ssm_a_re[0], ssm_a_im[0], ssm_log_dt[0], ssm_b_re[0], ssm_b_im[0], ssm_c_re[0], ssm_c_im[0])
    ut = _ssm_in(h, ssm_w_in[0].T.astype(BF16), _row_tile(seq, 1024))
    yt = _ssm_core(ut, kvec, pmat, cmat, lam_a, lam_b, seq // SSM_CHUNK)
    h = _ssm_out(yt, ut, ssm_d[0], ssm_w_glu[0].T.astype(BF16), ssm_b_glu[0],
                 ssm_w_out[0].astype(BF16), h, ln_g[1, 0], ln_b[1, 0], _row_tile(seq, 512))
    h = ffn(h, 1)
    return h.reshape(batch, seq, d_model)
```

```python
import functools
import math

import jax
import jax.numpy as jnp
from jax import lax
from jax.experimental import pallas as pl
from jax.experimental.pallas import tpu as pltpu

F32 = jnp.float32
BF16 = jnp.bfloat16

DEPTH = 2
DILATION_PAIRS = ((128, 1), (512, 4), (2048, 16))
N_GROUPS = len(DILATION_PAIRS)
HEADS = 8
HEAD_DIM = 128
ROPE_THETA = 10000.0
SSM_GROUP_CH = 16
SSM_STATE = 64
CONV_WIDTH = 3
DEEPNORM_ALPHA = (2.0 * DEPTH) ** 0.25
LN_EPS = 1e-5
NEG_INF = -1e30
LOG2E = math.log2(math.e)
LN2 = math.log(2.0)

LANES = 128
SUBLANES = 8
VMEM_LIMIT_BYTES = 56 * 1024 * 1024

ATTN_BLOCK = 128
SSM_CHUNK = 128


def _params(*semantics):
    return pltpu.CompilerParams(dimension_semantics=semantics,
                                vmem_limit_bytes=VMEM_LIMIT_BYTES)


def _layer_norm(r, g, b):
    mu = jnp.mean(r, axis=-1, keepdims=True)
    d = r - mu
    var = jnp.mean(d * d, axis=-1, keepdims=True)
    return d * lax.rsqrt(var + LN_EPS) * g + b


def _dot(a, b):
    return jnp.dot(a, b, preferred_element_type=F32)


def _dot_nt(a, b):
    return lax.dot_general(a, b, (((1,), (1,)), ((), ())), preferred_element_type=F32)


def _dot_tn(a, b):
    return lax.dot_general(a, b, (((0,), (0,)), ((), ())), preferred_element_type=F32)


def _residue_rows(r, n, d):
    return pl.ds(r, n, stride=d) if d > 1 else slice(None)


def _qkv_kernel(*refs, d):
    n_slabs = len(refs) - 7
    x_refs = refs[:n_slabs]
    w_ref, cos_ref, sin_ref, o_ref, xb_ref, cs_ref, sn_ref = refs[n_slabs:]
    tm = cos_ref.shape[0]
    n = tm // d
    hw = HEADS * HEAD_DIM

    for r in range(d):
        rows = _residue_rows(r, n, d)
        for c, x_ref in enumerate(x_refs):
            xb_ref[r * n:(r + 1) * n, c * LANES:(c + 1) * LANES] = x_ref[rows, :].astype(BF16)
        cs_ref[r * n:(r + 1) * n, :] = cos_ref[rows, :]
        sn_ref[r * n:(r + 1) * n, :] = sin_ref[rows, :]
    xb = xb_ref[...]

    for kind in range(3):
        acc = _dot(xb, w_ref[:, kind * hw:(kind + 1) * hw])
        if kind == 2:
            for r in range(d):
                o_ref[r, :, kind * hw:(kind + 1) * hw] = acc[r * n:(r + 1) * n, :].astype(BF16)
            continue
        scale = HEAD_DIM ** -0.5 * LOG2E if kind == 0 else 1.0
        cos = cs_ref[...] * scale
        sin = sn_ref[...] * scale
        for h in range(HEADS):
            t = acc[:, h * HEAD_DIM:(h + 1) * HEAD_DIM]
            rot = (t * cos + pltpu.roll(t, HEAD_DIM // 2, axis=1) * sin).astype(BF16)
            lanes = slice(kind * hw + h * HEAD_DIM, kind * hw + (h + 1) * HEAD_DIM)
            for r in range(d):
                o_ref[r, :, lanes] = rot[r * n:(r + 1) * n, :]


def _qkv_rope(x2d, w_in_bf16, cos_tab, sin_tab, g, d, batch, seq, tm):
    d_model = x2d.shape[1]
    hw = HEADS * HEAD_DIM
    seq_tiles = seq // tm
    n = tm // d
    return pl.pallas_call(
        functools.partial(_qkv_kernel, d=d),
        out_shape=jax.ShapeDtypeStruct((batch, d, seq // d, 3 * hw), BF16),
        grid=(batch * seq_tiles,),
        in_specs=[pl.BlockSpec((tm, LANES), lambda i, c=c: (i, c))
                  for c in range(d_model // LANES)] + [
            pl.BlockSpec((d_model, 3 * hw), lambda i: (0, g)),
            pl.BlockSpec((tm, HEAD_DIM), lambda i: (i % seq_tiles, 0)),
            pl.BlockSpec((tm, HEAD_DIM), lambda i: (i % seq_tiles, 0)),
        ],
        out_specs=pl.BlockSpec((None, d, n, 3 * hw),
                               lambda i: (i // seq_tiles, 0, i % seq_tiles, 0)),
        scratch_shapes=[pltpu.VMEM((tm, d_model), BF16),
                        pltpu.VMEM((tm, HEAD_DIM), F32),
                        pltpu.VMEM((tm, HEAD_DIM), F32)],
        compiler_params=_params("parallel"),
        name=f"qkv_rope_g{g}",
    )(*([x2d] * (d_model // LANES)), w_in_bf16, cos_tab, sin_tab)


def _attn_kernel(q_ref, k_ref, v_ref, kp_ref, vp_ref, o_ref, lse_ref, *, nblk):
    m = pl.program_id(2)
    blk_rows = ATTN_BLOCK
    row = lax.broadcasted_iota(jnp.int32, (blk_rows, 2 * blk_rows), 0)
    col = lax.broadcasted_iota(jnp.int32, (blk_rows, 2 * blk_rows), 1)
    band = jnp.logical_and(col >= row, col <= row + blk_rows)
    first_band = jnp.logical_and(band, jnp.logical_or(col >= blk_rows, m > 0))
    lane = lax.broadcasted_iota(jnp.int32, (blk_rows, LANES), 1)
    ones = jnp.ones((2 * blk_rows, HEAD_DIM), BF16)

    def block(q_rows, keys, vals, mask):
        lse_tile = jnp.zeros((blk_rows, LANES), F32)
        for h in range(HEADS):
            cols = slice(h * HEAD_DIM, (h + 1) * HEAD_DIM)
            s = jnp.where(mask, _dot_nt(q_ref[q_rows, cols], keys(cols)), NEG_INF)
            mx = jnp.max(jnp.maximum(s[:, :blk_rows], s[:, blk_rows:]), axis=1, keepdims=True)
            p = jnp.exp2(s - mx).astype(BF16)
            oe = _dot(p, jnp.concatenate([vals(cols), ones], axis=1))
            l = oe[:, HEAD_DIM:]
            o_ref[q_rows, cols] = (oe[:, :HEAD_DIM] / l).astype(BF16)
            lse_tile = jnp.where(lane == h, mx * LN2 + jnp.log(l), lse_tile)
        lse_ref[q_rows, :] = lse_tile

    head = slice(0, blk_rows)
    block(head,
          lambda cols: jnp.concatenate([kp_ref[:, cols], k_ref[head, cols]], axis=0),
          lambda cols: jnp.concatenate([vp_ref[:, cols], v_ref[head, cols]], axis=0),
          first_band)

    def body(blk, carry):
        q_rows = pl.ds(pl.multiple_of(blk * blk_rows, blk_rows), blk_rows)
        kv_rows = pl.ds(pl.multiple_of((blk - 1) * blk_rows, blk_rows), 2 * blk_rows)
        block(q_rows, lambda cols: k_ref[kv_rows, cols], lambda cols: v_ref[kv_rows, cols], band)
        return carry

    lax.fori_loop(1, nblk, body, 0)


def _dilated_attention_group(qkv_g, g, rows):
    batch, d, sub_len, _ = qkv_g.shape
    hw = HEADS * HEAD_DIM
    rows = min(rows, sub_len)
    nblk = rows // ATTN_BLOCK

    def main_spec(kind):
        return pl.BlockSpec((None, None, rows, hw), lambda b, r, m: (b, r, m, kind))

    def prev_spec(kind):
        return pl.BlockSpec((None, None, ATTN_BLOCK, hw),
                            lambda b, r, m: (b, r, jnp.maximum(m * nblk - 1, 0), kind))

    return pl.pallas_call(
        functools.partial(_attn_kernel, nblk=nblk),
        out_shape=(jax.ShapeDtypeStruct((batch, d, sub_len, hw), BF16),
                   jax.ShapeDtypeStruct((batch, d, sub_len, LANES), F32)),
        grid=(batch, d, sub_len // rows),
        in_specs=[main_spec(0), main_spec(1), main_spec(2), prev_spec(1), prev_spec(2)],
        out_specs=(pl.BlockSpec((None, None, rows, hw), lambda b, r, m: (b, r, m, 0)),
                   pl.BlockSpec((None, None, rows, LANES), lambda b, r, m: (b, r, m, 0))),
        compiler_params=_params("parallel", "parallel", "arbitrary"),
        name=f"dilated_attn_g{g}",
    )(qkv_g, qkv_g, qkv_g, qkv_g, qkv_g)


def _merge_out_kernel(o0_ref, o1_ref, o2_ref, l0_ref, l1_ref, l2_ref, e_ref, x_ref, w_ref,
                      g_ref, b_ref, out_ref, on_ref, ln_ref):
    tm = x_ref.shape[0]
    outs, lses = [], []
    for gi, (o_ref, l_ref) in enumerate(((o0_ref, l0_ref), (o1_ref, l1_ref), (o2_ref, l2_ref))):
        d = o_ref.shape[0]
        n = tm // d
        for r in range(d):
            rows = _residue_rows(r, n, d)
            o_r = o_ref[r].astype(F32)
            for c in range(on_ref.shape[1]):
                on_ref[gi, c, rows, :] = o_r[:, c * LANES:(c + 1) * LANES]
            ln_ref[gi, rows, :] = l_ref[r]
        outs.append(jnp.concatenate([on_ref[gi, c] for c in range(on_ref.shape[1])], axis=1))
        lses.append(ln_ref[gi])
    mx = jnp.maximum(jnp.maximum(lses[0], lses[1]), lses[2])
    ws = [jnp.exp(l - mx) for l in lses]
    inv = 1.0 / (ws[0] + ws[1] + ws[2])
    e = e_ref[...]
    merged = None
    for w, o in zip(ws, outs):
        w = w * inv
        w_hi = w.astype(BF16)
        w_lo = (w - w_hi.astype(F32)).astype(BF16)
        w_full = _dot(w_hi, e) + _dot(w_lo, e)
        term = w_full * o
        merged = term if merged is None else merged + term
    y = _dot(merged.astype(BF16), w_ref[...])
    r = DEEPNORM_ALPHA * x_ref[...] + y
    out_ref[...] = _layer_norm(r, g_ref[...], b_ref[...])


def _merge_out(os, lses, x2d, w_out_bf16, ln_g, ln_b, seq, tm):
    tokens, d_model = x2d.shape
    hw = HEADS * HEAD_DIM
    seq_tiles = seq // tm
    expand = (jnp.arange(LANES)[:, None] == (jnp.arange(hw)[None, :] // HEAD_DIM)).astype(BF16)
    row = lambda i: (i, 0)
    fixed = lambda i: (0, 0)

    def group_spec(arr):
        d, width = arr.shape[1], arr.shape[3]
        return pl.BlockSpec((None, d, tm // d, width),
                            lambda i: (i // seq_tiles, 0, i % seq_tiles, 0))

    return pl.pallas_call(
        _merge_out_kernel,
        out_shape=jax.ShapeDtypeStruct((tokens, d_model), F32),
        grid=(tokens // tm,),
        in_specs=[group_spec(a) for a in os] + [group_spec(a) for a in lses] + [
            pl.BlockSpec((LANES, hw), fixed),
            pl.BlockSpec((tm, d_model), row),
            pl.BlockSpec((hw, d_model), fixed),
            pl.BlockSpec((1, d_model), fixed),
            pl.BlockSpec((1, d_model), fixed),
        ],
        out_specs=pl.BlockSpec((tm, d_model), row),
        scratch_shapes=[pltpu.VMEM((N_GROUPS, hw // LANES, tm, LANES), F32),
                        pltpu.VMEM((N_GROUPS, tm, LANES), F32)],
        compiler_params=_params("parallel"),
        name="attn_merge_out_ln",
    )(*os, *lses, expand, x2d, w_out_bf16, ln_g.reshape(1, -1), ln_b.reshape(1, -1))


def _ffn_kernel(x_ref, halo_ref, wup_ref, cw_ref, cb_ref, wdown_ref, g_ref, b_ref, out_ref,
                xe_ref, h_ref, *, seq_tiles, d_ff, chunk):
    i = pl.program_id(0)
    x = x_ref[...]
    first = (i % seq_tiles) == 0
    xe_ref[0:SUBLANES, :] = jnp.where(first, 0.0, halo_ref[...]).astype(BF16)
    xe_ref[SUBLANES:, :] = x.astype(BF16)
    xe = xe_ref[...]
    for c in range(d_ff // chunk):
        gcols = slice(c * chunk, (c + 1) * chunk)
        vcols = slice(d_ff + c * chunk, d_ff + (c + 1) * chunk)
        gate = _dot(xe, wup_ref[:, gcols])
        val = _dot(xe[SUBLANES:, :], wup_ref[:, vcols])
        conv = (cb_ref[:, gcols]
                + cw_ref[0:1, gcols] * pltpu.roll(gate, 2, axis=0)[SUBLANES:, :]
                + cw_ref[1:2, gcols] * pltpu.roll(gate, 1, axis=0)[SUBLANES:, :]
                + cw_ref[2:3, gcols] * gate[SUBLANES:, :])
        act = conv * (1.0 / (1.0 + jnp.exp(-conv)))
        h_ref[:, gcols] = (act * val).astype(BF16)
    f = _dot(h_ref[...], wdown_ref[...])
    r = DEEPNORM_ALPHA * x + f
    out_ref[...] = _layer_norm(r, g_ref[...], b_ref[...])


def _conv_ffn(x2d, w_up_bf16, conv_w, conv_b, w_down_bf16, ln_g, ln_b, seq, tm, chunk):
    tokens, d_model = x2d.shape
    d_ff = w_down_bf16.shape[0]
    seq_tiles = seq // tm
    halo_blocks = tm // SUBLANES
    fixed = lambda i: (0, 0)
    single = dict(pipeline_mode=pl.Buffered(1))
    return pl.pallas_call(
        functools.partial(_ffn_kernel, seq_tiles=seq_tiles, d_ff=d_ff, chunk=chunk),
        out_shape=jax.ShapeDtypeStruct((tokens, d_model), F32),
        grid=(tokens // tm,),
        in_specs=[
            pl.BlockSpec((tm, d_model), lambda i: (i, 0)),
            pl.BlockSpec((SUBLANES, d_model), lambda i: (jnp.maximum(i * halo_blocks - 1, 0), 0)),
            pl.BlockSpec((d_model, 2 * d_ff), fixed, **single),
            pl.BlockSpec((CONV_WIDTH, d_ff), fixed),
            pl.BlockSpec((1, d_ff), fixed),
            pl.BlockSpec((d_ff, d_model), fixed, **single),
            pl.BlockSpec((1, d_model), fixed),
            pl.BlockSpec((1, d_model), fixed),
        ],
        out_specs=pl.BlockSpec((tm, d_model), lambda i: (i, 0)),
        scratch_shapes=[pltpu.VMEM((tm + SUBLANES, d_model), BF16),
                        pltpu.VMEM((tm, d_ff), BF16)],
        compiler_params=_params("parallel"),
        name="conv_ffn_ln",
    )(x2d, x2d, w_up_bf16, conv_w, conv_b.reshape(1, -1), w_down_bf16,
      ln_g.reshape(1, -1), ln_b.reshape(1, -1))


def _ssm_in_kernel(x_ref, wt_ref, ut_ref):
    u = _dot_nt(wt_ref[...], x_ref[...].astype(BF16))
    for q in range(ut_ref.shape[0]):
        ut_ref[q] = u[:, q * SSM_CHUNK:(q + 1) * SSM_CHUNK]


def _ssm_in(x2d, w_in_t_bf16, tn):
    tokens, d_model = x2d.shape
    d_ssm = w_in_t_bf16.shape[0]
    return pl.pallas_call(
        _ssm_in_kernel,
        out_shape=jax.ShapeDtypeStruct((tokens // SSM_CHUNK, d_ssm, SSM_CHUNK), F32),
        grid=(tokens // tn,),
        in_specs=[pl.BlockSpec((tn, d_model), lambda i: (i, 0)),
                  pl.BlockSpec((d_ssm, d_model), lambda i: (0, 0))],
        out_specs=pl.BlockSpec((tn // SSM_CHUNK, d_ssm, SSM_CHUNK), lambda i: (i, 0, 0)),
        compiler_params=_params("parallel"),
        name="ssm_in_proj",
    )(x2d, w_in_t_bf16)


def _ssm_core_kernel(ut_ref, kv_ref, pm_ref, cm_ref, la_ref, lb_ref, yt_ref, toep_ref, lhs_ref,
                     *, chunks_per_seq):
    gc = SSM_GROUP_CH
    L = SSM_CHUNK
    rows = ut_ref.shape[0]
    n2 = 2 * SSM_STATE

    u = pltpu.einshape("jcs->cjs", ut_ref[...])
    for c in range(gc):
        lhs_ref[:, c * L:(c + 1) * L] = u[c].astype(BF16)

    srow = lax.broadcasted_iota(jnp.int32, (L, L), 0)
    tcol = lax.broadcasted_iota(jnp.int32, (L, L), 1)
    causal = tcol >= srow

    def build(c, carry):
        for cp in range(gc):
            kv = kv_ref[c, pl.ds(cp, 1), :]
            blk = pltpu.roll(jnp.broadcast_to(kv, (L, L)), 0, 1, stride=1, stride_axis=0)
            blk = jnp.where(causal, blk, 0.0)
            toep_ref[pl.ds(pl.multiple_of(c * L, L), L), cp * L:(cp + 1) * L] = blk.astype(BF16)
        return carry

    lax.fori_loop(0, gc, build, 0)

    lhs = lhs_ref[...]
    y = _dot(lhs, toep_ref[...])
    st = _dot(lhs, pm_ref[...])

    a = la_ref[...]
    b = lb_ref[...]
    jrow = lax.broadcasted_iota(jnp.int32, (rows, n2), 0) % chunks_per_seq
    sh = 1
    while sh < chunks_per_seq:
        prev = jnp.where(jrow >= sh, pltpu.roll(st, sh, axis=0), 0.0)
        st = st + prev * a + pltpu.roll(prev, SSM_STATE, axis=1) * b
        a, b = a * a - b * b, 2.0 * a * b
        sh *= 2
    carried = jnp.where(jrow >= 1, pltpu.roll(st, 1, axis=0), 0.0)
    y = y + _dot(carried.astype(BF16), cm_ref[...])

    yt_ref[...] = pltpu.einshape(
        "cjs->jcs", jnp.stack([y[:, cp * L:(cp + 1) * L] for cp in range(gc)], axis=0))


def _ssm_core(ut, kvec, pmat, cmat, lam_a, lam_b, chunks_per_seq):
    rows, d_ssm, L = ut.shape
    gc, n2 = SSM_GROUP_CH, 2 * SSM_STATE
    return pl.pallas_call(
        functools.partial(_ssm_core_kernel, chunks_per_seq=chunks_per_seq),
        out_shape=jax.ShapeDtypeStruct((rows, d_ssm, L), F32),
        grid=(d_ssm // gc,),
        in_specs=[
            pl.BlockSpec((rows, gc, L), lambda g: (0, g, 0)),
            pl.BlockSpec((None, gc, gc, L), lambda g: (g, 0, 0, 0)),
            pl.BlockSpec((None, gc * L, n2), lambda g: (g, 0, 0)),
            pl.BlockSpec((None, n2, gc * L), lambda g: (g, 0, 0)),
            pl.BlockSpec((None, 1, n2), lambda g: (g, 0, 0)),
            pl.BlockSpec((None, 1, n2), lambda g: (g, 0, 0)),
        ],
        out_specs=pl.BlockSpec((rows, gc, L), lambda g: (0, g, 0)),
        scratch_shapes=[pltpu.VMEM((gc * L, gc * L), BF16),
                        pltpu.VMEM((rows, gc * L), BF16)],
        compiler_params=_params("parallel"),
        name="ssm_chunk_conv",
    )(ut, kvec, pmat, cmat, lam_a, lam_b)


def _ssm_operators(a_re, a_im, log_dt, b_re, b_im, c_re, c_im):
    L = SSM_CHUNK
    hi = lax.Precision.HIGHEST
    a_re = a_re.astype(F32)
    a_im = a_im.astype(F32)
    dt = jnp.exp(log_dt.astype(F32))[:, None]
    mag = jnp.exp(a_re * dt)
    lam_re = mag * jnp.cos(a_im * dt)
    lam_im = mag * jnp.sin(a_im * dt)
    nr, ni = lam_re - 1.0, lam_im
    den = a_re * a_re + a_im * a_im
    coef_re = ((nr * a_re + ni * a_im) / den)[..., None]
    coef_im = ((ni * a_re - nr * a_im) / den)[..., None]
    b_re = b_re.astype(F32)
    b_im = b_im.astype(F32)
    bb_re = coef_re * b_re - coef_im * b_im
    bb_im = coef_re * b_im + coef_im * b_re
    cr = jnp.swapaxes(c_re.astype(F32), 1, 2)
    ci = jnp.swapaxes(c_im.astype(F32), 1, 2)

    k = jnp.arange(L + 1, dtype=F32)
    pmag = jnp.exp((a_re * dt)[..., None] * k)
    ang = (a_im * dt)[..., None] * k
    pw_re = pmag * jnp.cos(ang)
    pw_im = pmag * jnp.sin(ang)

    cb_re = cr[:, :, None, :] * bb_re[..., None] - ci[:, :, None, :] * bb_im[..., None]
    cb_im = cr[:, :, None, :] * bb_im[..., None] + ci[:, :, None, :] * bb_re[..., None]
    kvec = (jnp.einsum('gncd,gnk->gcdk', cb_re, pw_re[..., :L], precision=hi)
            - jnp.einsum('gncd,gnk->gcdk', cb_im, pw_im[..., :L], precision=hi))

    rp_re = jnp.flip(pw_re[..., :L], axis=-1)
    rp_im = jnp.flip(pw_im[..., :L], axis=-1)
    pm_re = rp_re[:, :, None, :] * bb_re[..., None] - rp_im[:, :, None, :] * bb_im[..., None]
    pm_im = rp_re[:, :, None, :] * bb_im[..., None] + rp_im[:, :, None, :] * bb_re[..., None]
    pmat = jnp.concatenate([pm_re, pm_im], axis=1)
    g_, n2, c_, _ = pmat.shape
    pmat = jnp.transpose(pmat, (0, 2, 3, 1)).reshape(g_, c_ * L, n2).astype(BF16)

    fp_re = pw_re[..., 1:]
    fp_im = pw_im[..., 1:]
    cm_re = cr[..., None] * fp_re[:, :, None, :] - ci[..., None] * fp_im[:, :, None, :]
    cm_im = -(cr[..., None] * fp_im[:, :, None, :] + ci[..., None] * fp_re[:, :, None, :])
    cmat = jnp.concatenate([cm_re, cm_im], axis=1).reshape(g_, n2, c_ * L).astype(BF16)

    lam_a = jnp.concatenate([pw_re[..., L], pw_re[..., L]], axis=-1)[:, None, :]
    lam_b = jnp.concatenate([-pw_im[..., L], pw_im[..., L]], axis=-1)[:, None, :]
    return kvec, pmat, cmat, lam_a, lam_b


def _ssm_out_kernel(yt_ref, ut_ref, d_ref, wg_ref, bg_ref, wo_ref, x_ref, g_ref, b_ref, out_ref):
    nq = yt_ref.shape[0]
    yt = jnp.concatenate([yt_ref[q] for q in range(nq)], axis=1)
    ut = jnp.concatenate([ut_ref[q] for q in range(nq)], axis=1)
    y = yt + d_ref[...] * ut
    z = 0.5 * y * (1.0 + jnp.tanh(math.sqrt(2.0 / math.pi) * (y + 0.044715 * (y * y * y))))
    gate = _dot(wg_ref[...], z.astype(BF16)) + bg_ref[...]
    zs = z * (1.0 / (1.0 + jnp.exp(-gate)))
    f = _dot_tn(zs.astype(BF16), wo_ref[...])
    r = DEEPNORM_ALPHA * x_ref[...] + f
    out_ref[...] = _layer_norm(r, g_ref[...], b_ref[...])


def _ssm_out(yt, ut, d_skip, w_glu_t_bf16, b_glu, w_out_bf16, x2d, ln_g, ln_b, tn):
    tokens, d_model = x2d.shape
    d_ssm = yt.shape[1]
    nq = tn // SSM_CHUNK
    fixed = lambda i: (0, 0)
    chunked = pl.BlockSpec((nq, d_ssm, SSM_CHUNK), lambda i: (i, 0, 0))
    return pl.pallas_call(
        _ssm_out_kernel,
        out_shape=jax.ShapeDtypeStruct((tokens, d_model), F32),
        grid=(tokens // tn,),
        in_specs=[
            chunked,
            chunked,
            pl.BlockSpec((d_ssm, 1), fixed),
            pl.BlockSpec((d_ssm, d_ssm), fixed),
            pl.BlockSpec((d_ssm, 1), fixed),
            pl.BlockSpec((d_ssm, d_model), fixed),
            pl.BlockSpec((tn, d_model), lambda i: (i, 0)),
            pl.BlockSpec((1, d_model), fixed),
            pl.BlockSpec((1, d_model), fixed),
        ],
        out_specs=pl.BlockSpec((tn, d_model), lambda i: (i, 0)),
        compiler_params=_params("parallel"),
        name="ssm_glu_out_ln",
    )(yt, ut, d_skip.reshape(-1, 1), w_glu_t_bf16, b_glu.reshape(-1, 1), w_out_bf16, x2d,
      ln_g.reshape(1, -1), ln_b.reshape(1, -1))


def _rope_tables(seq):
    pos = jnp.arange(seq, dtype=F32)
    inv_freq = ROPE_THETA ** (-jnp.arange(0, HEAD_DIM, 2, dtype=F32) / HEAD_DIM)
    ang = pos[:, None] * inv_freq[None, :]
    cos = jnp.cos(ang)
    sin = jnp.sin(ang)
    return jnp.concatenate([cos, cos], axis=1), jnp.concatenate([-sin, sin], axis=1)


def _row_tile(seq, want):
    tm = min(want, seq)
    assert seq % tm == 0
    return tm


def kernel(x, attn_w_in, attn_w_out, ssm_w_in, ssm_a_re, ssm_a_im, ssm_log_dt, ssm_b_re, ssm_b_im,
           ssm_c_re, ssm_c_im, ssm_d, ssm_w_glu, ssm_b_glu, ssm_w_out, ffn_w_up, ffn_conv_w,
           ffn_conv_b, ffn_w_down, ln_g, ln_b):
    batch, seq, d_model = x.shape
    assert seq % DILATION_PAIRS[-1][0] == 0 and seq % SSM_CHUNK == 0
    assert all(w // d == ATTN_BLOCK for w, d in DILATION_PAIRS)
    h = x.reshape(batch * seq, d_model)
    ffn_chunk = 256

    def ffn(h, i):
        return _conv_ffn(h, ffn_w_up[i].astype(BF16), ffn_conv_w[i], ffn_conv_b[i],
                         ffn_w_down[i].astype(BF16), ln_g[i, 1], ln_b[i, 1], seq,
                         _row_tile(seq, 1024), ffn_chunk)

    cos_tab, sin_tab = _rope_tables(seq)
    w_in = attn_w_in[0].astype(BF16)
    os, lses = [], []
    for g, (_, dilation) in enumerate(DILATION_PAIRS):
        qkv_g = _qkv_rope(h, w_in, cos_tab, sin_tab, g, dilation, batch, seq, _row_tile(seq, 1024))
        o, lse = _dilated_attention_group(qkv_g, g, 1024)
        os.append(o)
        lses.append(lse)
    h = _merge_out(os, lses, h, attn_w_out[0].astype(BF16), ln_g[0, 0], ln_b[0, 0], seq,
                   _row_tile(seq, 512))
    h = ffn(h, 0)

    kvec, pmat, cmat, lam_a, lam_b = _ssm_operators(
        ssm_a_re[0], ssm_a_im[0], ssm_log_dt[0], ssm_b_re[0], ssm_b_im[0], ssm_c_re[0], ssm_c_im[0])
    ut = _ssm_in(h, ssm_w_in[0].T.astype(BF16), _row_tile(seq, 1024))
    yt = _ssm_core(ut, kvec, pmat, cmat, lam_a, lam_b, seq // SSM_CHUNK)
    h = _ssm_out(yt, ut, ssm_d[0], ssm_w_glu[0].T.astype(BF16), ssm_b_glu[0],
                 ssm_w_out[0].astype(BF16), h, ln_g[1, 0], ln_b[1, 0], _row_tile(seq, 512))
    h = ffn(h, 1)
    return h.reshape(batch, seq, d_model)
```

```python
import functools
import math

import jax
import jax.numpy as jnp
from jax import lax
from jax.experimental import pallas as pl
from jax.experimental.pallas import tpu as pltpu

F32 = jnp.float32
BF16 = jnp.bfloat16

DEPTH = 2
DILATION_PAIRS = ((128, 1), (512, 4), (2048, 16))
N_GROUPS = len(DILATION_PAIRS)
HEADS = 8
HEAD_DIM = 128
ROPE_THETA = 10000.0
SSM_GROUP_CH = 16
SSM_STATE = 64
CONV_WIDTH = 3
DEEPNORM_ALPHA = (2.0 * DEPTH) ** 0.25
LN_EPS = 1e-5
NEG_INF = -1e30
LOG2E = math.log2(math.e)
LN2 = math.log(2.0)

LANES = 128
SUBLANES = 8
VMEM_LIMIT_BYTES = 56 * 1024 * 1024

ATTN_BLOCK = 128
SSM_CHUNK = 128


def _params(*semantics):
    return pltpu.CompilerParams(dimension_semantics=semantics,
                                vmem_limit_bytes=VMEM_LIMIT_BYTES)


def _layer_norm(r, g, b):
    mu = jnp.mean(r, axis=-1, keepdims=True)
    d = r - mu
    var = jnp.mean(d * d, axis=-1, keepdims=True)
    return d * lax.rsqrt(var + LN_EPS) * g + b


def _dot(a, b):
    return jnp.dot(a, b, preferred_element_type=F32)


def _dot_nt(a, b):
    return lax.dot_general(a, b, (((1,), (1,)), ((), ())), preferred_element_type=F32)


def _dot_tn(a, b):
    return lax.dot_general(a, b, (((0,), (0,)), ((), ())), preferred_element_type=F32)


def _residue_rows(r, n, d):
    return pl.ds(r, n, stride=d) if d > 1 else slice(None)


def _qkv_kernel(*refs, d):
    n_slabs = len(refs) - 8
    x_refs = refs[:n_slabs]
    w_ref, cos_ref, sin_ref, o_ref, xb_ref, cs_ref, sn_ref, tmp_ref = refs[n_slabs:]
    tm = cos_ref.shape[0]
    n = tm // d
    hw = HEADS * HEAD_DIM

    def residues(src_ref):
        if d % 8:
            return [src_ref[_residue_rows(r, n, d), :] for r in range(d)]
        for a in range(4):
            tmp_ref[a] = src_ref[pl.ds(a, tm // 4, stride=4), :]
        return [tmp_ref[r % 4, pl.ds(r // 4, n, stride=d // 4), :] for r in range(d)]

    for c, x_ref in enumerate(x_refs):
        for r, piece in enumerate(residues(x_ref)):
            xb_ref[r * n:(r + 1) * n, c * LANES:(c + 1) * LANES] = piece.astype(BF16)
    for src_ref, dst_ref in ((cos_ref, cs_ref), (sin_ref, sn_ref)):
        for r, piece in enumerate(residues(src_ref)):
            dst_ref[r * n:(r + 1) * n, :] = piece
    xb = xb_ref[...]

    for kind in range(3):
        acc = _dot(xb, w_ref[:, kind * hw:(kind + 1) * hw])
        if kind == 2:
            for r in range(d):
                o_ref[r, :, kind * hw:(kind + 1) * hw] = acc[r * n:(r + 1) * n, :].astype(BF16)
            continue
        scale = HEAD_DIM ** -0.5 * LOG2E if kind == 0 else 1.0
        cos = cs_ref[...] * scale
        sin = sn_ref[...] * scale
        for h in range(HEADS):
            t = acc[:, h * HEAD_DIM:(h + 1) * HEAD_DIM]
            rot = (t * cos + pltpu.roll(t, HEAD_DIM // 2, axis=1) * sin).astype(BF16)
            lanes = slice(kind * hw + h * HEAD_DIM, kind * hw + (h + 1) * HEAD_DIM)
            for r in range(d):
                o_ref[r, :, lanes] = rot[r * n:(r + 1) * n, :]


def _qkv_rope(x2d, w_in_bf16, cos_tab, sin_tab, g, d, batch, seq, tm):
    d_model = x2d.shape[1]
    hw = HEADS * HEAD_DIM
    seq_tiles = seq // tm
    n = tm // d
    return pl.pallas_call(
        functools.partial(_qkv_kernel, d=d),
        out_shape=jax.ShapeDtypeStruct((batch, d, seq // d, 3 * hw), BF16),
        grid=(batch * seq_tiles,),
        in_specs=[pl.BlockSpec((tm, LANES), lambda i, c=c: (i, c))
                  for c in range(d_model // LANES)] + [
            pl.BlockSpec((d_model, 3 * hw), lambda i: (0, g)),
            pl.BlockSpec((tm, HEAD_DIM), lambda i: (i % seq_tiles, 0)),
            pl.BlockSpec((tm, HEAD_DIM), lambda i: (i % seq_tiles, 0)),
        ],
        out_specs=pl.BlockSpec((None, d, n, 3 * hw),
                               lambda i: (i // seq_tiles, 0, i % seq_tiles, 0)),
        scratch_shapes=[pltpu.VMEM((tm, d_model), BF16),
                        pltpu.VMEM((tm, HEAD_DIM), F32),
                        pltpu.VMEM((tm, HEAD_DIM), F32),
                        pltpu.VMEM((4, tm // 4, LANES), F32)],
        compiler_params=_params("parallel"),
        name=f"qkv_rope_g{g}",
    )(*([x2d] * (d_model // LANES)), w_in_bf16, cos_tab, sin_tab)


def _attn_kernel(q_ref, k_ref, v_ref, kp_ref, vp_ref, o_ref, lse_ref, *, nblk):
    m = pl.program_id(2)
    blk_rows = ATTN_BLOCK
    row = lax.broadcasted_iota(jnp.int32, (blk_rows, 2 * blk_rows), 0)
    col = lax.broadcasted_iota(jnp.int32, (blk_rows, 2 * blk_rows), 1)
    band = jnp.logical_and(col >= row, col <= row + blk_rows)
    first_band = jnp.logical_and(band, jnp.logical_or(col >= blk_rows, m > 0))
    lane = lax.broadcasted_iota(jnp.int32, (blk_rows, LANES), 1)
    ones = jnp.ones((2 * blk_rows, HEAD_DIM), BF16)

    def block(q_rows, keys, vals, mask):
        mx_tile = jnp.zeros((blk_rows, LANES), F32)
        l_tile = jnp.ones((blk_rows, LANES), F32)
        for h in range(HEADS):
            cols = slice(h * HEAD_DIM, (h + 1) * HEAD_DIM)
            s = jnp.where(mask, _dot_nt(q_ref[q_rows, cols], keys(cols)), NEG_INF)
            mx = jnp.max(jnp.maximum(s[:, :blk_rows], s[:, blk_rows:]), axis=1, keepdims=True)
            p = jnp.exp2(s - mx).astype(BF16)
            oe = _dot(p, jnp.concatenate([vals(cols), ones], axis=1))
            l = oe[:, HEAD_DIM:]
            o_ref[q_rows, cols] = (oe[:, :HEAD_DIM] / l).astype(BF16)
            mx_tile = jnp.where(lane == h, mx, mx_tile)
            l_tile = jnp.where(lane == h, l, l_tile)
        lse_ref[q_rows, :] = mx_tile * LN2 + jnp.log(l_tile)

    head = slice(0, blk_rows)
    block(head,
          lambda cols: jnp.concatenate([kp_ref[:, cols], k_ref[head, cols]], axis=0),
          lambda cols: jnp.concatenate([vp_ref[:, cols], v_ref[head, cols]], axis=0),
          first_band)

    def body(blk, carry):
        q_rows = pl.ds(pl.multiple_of(blk * blk_rows, blk_rows), blk_rows)
        kv_rows = pl.ds(pl.multiple_of((blk - 1) * blk_rows, blk_rows), 2 * blk_rows)
        block(q_rows, lambda cols: k_ref[kv_rows, cols], lambda cols: v_ref[kv_rows, cols], band)
        return carry

    lax.fori_loop(1, nblk, body, 0, unroll=True)


def _dilated_attention_group(qkv_g, g, rows):
    batch, d, sub_len, _ = qkv_g.shape
    hw = HEADS * HEAD_DIM
    rows = min(rows, sub_len)
    nblk = rows // ATTN_BLOCK

    def main_spec(kind):
        return pl.BlockSpec((None, None, rows, hw), lambda b, r, m: (b, r, m, kind))

    def prev_spec(kind):
        return pl.BlockSpec((None, None, ATTN_BLOCK, hw),
                            lambda b, r, m: (b, r, jnp.maximum(m * nblk - 1, 0), kind))

    return pl.pallas_call(
        functools.partial(_attn_kernel, nblk=nblk),
        out_shape=(jax.ShapeDtypeStruct((batch, d, sub_len, hw), BF16),
                   jax.ShapeDtypeStruct((batch, d, sub_len, LANES), F32)),
        grid=(batch, d, sub_len // rows),
        in_specs=[main_spec(0), main_spec(1), main_spec(2), prev_spec(1), prev_spec(2)],
        out_specs=(pl.BlockSpec((None, None, rows, hw), lambda b, r, m: (b, r, m, 0)),
                   pl.BlockSpec((None, None, rows, LANES), lambda b, r, m: (b, r, m, 0))),
        compiler_params=_params("parallel", "parallel", "arbitrary"),
        name=f"dilated_attn_g{g}",
    )(qkv_g, qkv_g, qkv_g, qkv_g, qkv_g)


def _merge_out_kernel(o0_ref, o1_ref, o2_ref, l0_ref, l1_ref, l2_ref, e_ref, x_ref, w_ref,
                      g_ref, b_ref, out_ref, on_ref, ln_ref):
    tm = x_ref.shape[0]
    outs, lses = [], []
    for gi, (o_ref, l_ref) in enumerate(((o0_ref, l0_ref), (o1_ref, l1_ref), (o2_ref, l2_ref))):
        d = o_ref.shape[0]
        n = tm // d
        for r in range(d):
            rows = _residue_rows(r, n, d)
            o_r = o_ref[r].astype(F32)
            for c in range(on_ref.shape[1]):
                on_ref[gi, c, rows, :] = o_r[:, c * LANES:(c + 1) * LANES]
            ln_ref[gi, rows, :] = l_ref[r]
        outs.append(jnp.concatenate([on_ref[gi, c] for c in range(on_ref.shape[1])], axis=1))
        lses.append(ln_ref[gi])
    mx = jnp.maximum(jnp.maximum(lses[0], lses[1]), lses[2])
    ws = [jnp.exp(l - mx) for l in lses]
    inv = 1.0 / (ws[0] + ws[1] + ws[2])
    e = e_ref[...]
    merged = None
    for w, o in zip(ws, outs):
        w = w * inv
        w_hi = w.astype(BF16)
        w_lo = (w - w_hi.astype(F32)).astype(BF16)
        w_full = _dot(jnp.concatenate([w_hi, w_lo], axis=1), e)
        term = w_full * o
        merged = term if merged is None else merged + term
    y = _dot(merged.astype(BF16), w_ref[...])
    r = DEEPNORM_ALPHA * x_ref[...] + y
    out_ref[...] = _layer_norm(r, g_ref[...], b_ref[...])


def _merge_out(os, lses, x2d, w_out_bf16, ln_g, ln_b, seq, tm):
    tokens, d_model = x2d.shape
    hw = HEADS * HEAD_DIM
    seq_tiles = seq // tm
    expand = (jnp.arange(2 * LANES)[:, None] % LANES == (jnp.arange(hw)[None, :] // HEAD_DIM)).astype(BF16)
    row = lambda i: (i, 0)
    fixed = lambda i: (0, 0)

    def group_spec(arr):
        d, width = arr.shape[1], arr.shape[3]
        return pl.BlockSpec((None, d, tm // d, width),
                            lambda i: (i // seq_tiles, 0, i % seq_tiles, 0))

    return pl.pallas_call(
        _merge_out_kernel,
        out_shape=jax.ShapeDtypeStruct((tokens, d_model), F32),
        grid=(tokens // tm,),
        in_specs=[group_spec(a) for a in os] + [group_spec(a) for a in lses] + [
            pl.BlockSpec((2 * LANES, hw), fixed),
            pl.BlockSpec((tm, d_model), row),
            pl.BlockSpec((hw, d_model), fixed),
            pl.BlockSpec((1, d_model), fixed),
            pl.BlockSpec((1, d_model), fixed),
        ],
        out_specs=pl.BlockSpec((tm, d_model), row),
        scratch_shapes=[pltpu.VMEM((N_GROUPS, hw // LANES, tm, LANES), F32),
                        pltpu.VMEM((N_GROUPS, tm, LANES), F32)],
        compiler_params=_params("parallel"),
        name="attn_merge_out_ln",
    )(*os, *lses, expand, x2d, w_out_bf16, ln_g.reshape(1, -1), ln_b.reshape(1, -1))


def _ffn_kernel(x_ref, halo_ref, wup_ref, cw_ref, cb_ref, wdown_ref, g_ref, b_ref, out_ref,
                xe_ref, h_ref, *, seq_tiles, d_ff, chunk):
    i = pl.program_id(0)
    x = x_ref[...]
    first = (i % seq_tiles) == 0
    xe_ref[0:SUBLANES, :] = jnp.where(first, 0.0, halo_ref[...]).astype(BF16)
    xe_ref[SUBLANES:, :] = x.astype(BF16)
    xe = xe_ref[...]
    for c in range(d_ff // chunk):
        gcols = slice(c * chunk, (c + 1) * chunk)
        vcols = slice(d_ff + c * chunk, d_ff + (c + 1) * chunk)
        gate = _dot(xe, wup_ref[:, gcols])
        val = _dot(xe[SUBLANES:, :], wup_ref[:, vcols])
        conv = (cb_ref[:, gcols]
                + cw_ref[0:1, gcols] * pltpu.roll(gate, 2, axis=0)[SUBLANES:, :]
                + cw_ref[1:2, gcols] * pltpu.roll(gate, 1, axis=0)[SUBLANES:, :]
                + cw_ref[2:3, gcols] * gate[SUBLANES:, :])
        act = conv * (1.0 / (1.0 + jnp.exp(-conv)))
        h_ref[:, gcols] = (act * val).astype(BF16)
    f = _dot(h_ref[...], wdown_ref[...])
    r = DEEPNORM_ALPHA * x + f
    out_ref[...] = _layer_norm(r, g_ref[...], b_ref[...])


def _conv_ffn(x2d, w_up_bf16, conv_w, conv_b, w_down_bf16, ln_g, ln_b, seq, tm, chunk):
    tokens, d_model = x2d.shape
    d_ff = w_down_bf16.shape[0]
    seq_tiles = seq // tm
    halo_blocks = tm // SUBLANES
    fixed = lambda i: (0, 0)
    single = dict(pipeline_mode=pl.Buffered(1))
    return pl.pallas_call(
        functools.partial(_ffn_kernel, seq_tiles=seq_tiles, d_ff=d_ff, chunk=chunk),
        out_shape=jax.ShapeDtypeStruct((tokens, d_model), F32),
        grid=(tokens // tm,),
        in_specs=[
            pl.BlockSpec((tm, d_model), lambda i: (i, 0)),
            pl.BlockSpec((SUBLANES, d_model), lambda i: (jnp.maximum(i * halo_blocks - 1, 0), 0)),
            pl.BlockSpec((d_model, 2 * d_ff), fixed, **single),
            pl.BlockSpec((CONV_WIDTH, d_ff), fixed),
            pl.BlockSpec((1, d_ff), fixed),
            pl.BlockSpec((d_ff, d_model), fixed, **single),
            pl.BlockSpec((1, d_model), fixed),
            pl.BlockSpec((1, d_model), fixed),
        ],
        out_specs=pl.BlockSpec((tm, d_model), lambda i: (i, 0)),
        scratch_shapes=[pltpu.VMEM((tm + SUBLANES, d_model), BF16),
                        pltpu.VMEM((tm, d_ff), BF16)],
        compiler_params=_params("parallel"),
        name="conv_ffn_ln",
    )(x2d, x2d, w_up_bf16, conv_w, conv_b.reshape(1, -1), w_down_bf16,
      ln_g.reshape(1, -1), ln_b.reshape(1, -1))


def _ssm_in_kernel(x_ref, wt_ref, ut_ref):
    gc = SSM_GROUP_CH
    u = _dot_nt(wt_ref[...], x_ref[...].astype(BF16))
    for g in range(ut_ref.shape[0]):
        for q in range(ut_ref.shape[1] // gc):
            ut_ref[g, q * gc:(q + 1) * gc, :] = u[g * gc:(g + 1) * gc, q * SSM_CHUNK:(q + 1) * SSM_CHUNK]


def _ssm_in(x2d, w_in_t_bf16, tn):
    tokens, d_model = x2d.shape
    d_ssm = w_in_t_bf16.shape[0]
    gc = SSM_GROUP_CH
    groups = d_ssm // gc
    return pl.pallas_call(
        _ssm_in_kernel,
        out_shape=jax.ShapeDtypeStruct((groups, tokens // SSM_CHUNK * gc, SSM_CHUNK), F32),
        grid=(tokens // tn,),
        in_specs=[pl.BlockSpec((tn, d_model), lambda i: (i, 0)),
                  pl.BlockSpec((d_ssm, d_model), lambda i: (0, 0))],
        out_specs=pl.BlockSpec((groups, tn // SSM_CHUNK * gc, SSM_CHUNK), lambda i: (0, i, 0)),
        compiler_params=_params("parallel"),
        name="ssm_in_proj",
    )(x2d, w_in_t_bf16)


def _ssm_core_kernel(ut_ref, kv0_ref, kvn_ref, rpt_ref, fpt_ref, ba_ref, bb_ref, ca_ref, cb_ref,
                     la_ref, lb_ref, yt_ref, toep0_ref, toep1_ref, lhs_ref, pm_ref, cmt_ref, yacc_ref,
                     tmp_ref, *, chunks_per_seq):
    gc = SSM_GROUP_CH
    L = SSM_CHUNK
    rows = ut_ref.shape[0] // gc
    n2 = 2 * SSM_STATE
    pair = 2 * L
    steps = gc // 2
    g = pl.program_id(0)
    slot = g % 2

    srow = lax.broadcasted_iota(jnp.int32, (L, L), 0)
    tcol = lax.broadcasted_iota(jnp.int32, (L, L), 1)
    causal = tcol >= srow

    def build_pair(kv_ref, i, dst_ref):
        for cc in range(2):
            c = 2 * i + cc
            for cp in range(gc):
                kv = kv_ref[c, pl.ds(cp, 1), :]
                blk = pltpu.roll(jnp.broadcast_to(kv, (L, L)), 0, 1, stride=1, stride_axis=0)
                blk = jnp.where(causal, blk, 0.0)
                dst_ref[pl.ds(pl.multiple_of(c * L, L), L), cp * L:(cp + 1) * L] = blk.astype(BF16)

    @pl.when(g == 0)
    def _():
        def first(i, carry):
            build_pair(kv0_ref, i, toep0_ref)
            return carry
        lax.fori_loop(0, steps, first, 0)

    quarter = rows * gc // 4
    for a in range(4):
        tmp_ref[a] = ut_ref[pl.ds(a, quarter, stride=4), :]
    for c in range(gc):
        piece = tmp_ref[c % 4, pl.ds(c // 4, rows, stride=4), :].astype(BF16)
        lhs_ref[c // 2, :, (c % 2) * L:(c % 2 + 1) * L] = piece

    rpt = rpt_ref[...]
    rpt_sw = pltpu.roll(rpt, SSM_STATE, axis=1)
    fpt = fpt_ref[...]
    fpt_sw = pltpu.roll(fpt, SSM_STATE, axis=1)
    for c in range(gc):
        pm_ref[c * L:(c + 1) * L, :] = (rpt * ba_ref[c:c + 1, :] + rpt_sw * bb_ref[c:c + 1, :]).astype(BF16)
        cmt_ref[c * L:(c + 1) * L, :] = (fpt * ca_ref[c:c + 1, :] + fpt_sw * cb_ref[c:c + 1, :]).astype(BF16)

    st = None
    for i in range(steps):
        part = _dot(lhs_ref[i], pm_ref[i * pair:(i + 1) * pair, :])
        st = part if st is None else st + part

    a = la_ref[...]
    b = lb_ref[...]
    jrow = lax.broadcasted_iota(jnp.int32, (rows, n2), 0) % chunks_per_seq
    sh = 1
    while sh < chunks_per_seq:
        prev = jnp.where(jrow >= sh, pltpu.roll(st, sh, axis=0), 0.0)
        st = st + prev * a + pltpu.roll(prev, SSM_STATE, axis=1) * b
        a, b = a * a - b * b, 2.0 * a * b
        sh *= 2
    carried = jnp.where(jrow >= 1, pltpu.roll(st, 1, axis=0), 0.0)
    yacc_ref[...] = _dot_nt(carried.astype(BF16), cmt_ref[...])

    def pipeline(cur_ref, nxt_ref):
        def step(i, carry):
            build_pair(kvn_ref, i, nxt_ref)
            rws = pl.ds(pl.multiple_of(i * pair, pair), pair)
            yacc_ref[...] += _dot(lhs_ref[i], cur_ref[rws, :])
            return carry
        lax.fori_loop(0, steps, step, 0)

    @pl.when(slot == 0)
    def _():
        pipeline(toep0_ref, toep1_ref)

    @pl.when(slot == 1)
    def _():
        pipeline(toep1_ref, toep0_ref)

    for cp in range(gc):
        tmp_ref[cp % 4, pl.ds(cp // 4, rows, stride=4), :] = yacc_ref[:, cp * L:(cp + 1) * L]
    for a4 in range(4):
        yt_ref[pl.ds(a4, quarter, stride=4), :] = tmp_ref[a4]


def _ssm_core(ut, kvec, rpt, fpt, ba, bb, ca, cb, lam_a, lam_b, chunks_per_seq):
    groups, group_rows, L = ut.shape
    gc, n2 = SSM_GROUP_CH, 2 * SSM_STATE
    rows = group_rows // gc
    per_group = lambda g: (g, 0, 0)
    return pl.pallas_call(
        functools.partial(_ssm_core_kernel, chunks_per_seq=chunks_per_seq),
        out_shape=jax.ShapeDtypeStruct((groups, group_rows, L), F32),
        grid=(groups,),
        in_specs=[
            pl.BlockSpec((None, group_rows, L), per_group),
            pl.BlockSpec((None, gc, gc, L), lambda g: (0, 0, 0, 0)),
            pl.BlockSpec((None, gc, gc, L), lambda g: (jnp.minimum(g + 1, groups - 1), 0, 0, 0)),
            pl.BlockSpec((None, L, n2), per_group),
            pl.BlockSpec((None, L, n2), per_group),
            pl.BlockSpec((None, gc, n2), per_group),
            pl.BlockSpec((None, gc, n2), per_group),
            pl.BlockSpec((None, gc, n2), per_group),
            pl.BlockSpec((None, gc, n2), per_group),
            pl.BlockSpec((None, 1, n2), per_group),
            pl.BlockSpec((None, 1, n2), per_group),
        ],
        out_specs=pl.BlockSpec((None, group_rows, L), per_group),
        scratch_shapes=[pltpu.VMEM((gc * L, gc * L), BF16),
                        pltpu.VMEM((gc * L, gc * L), BF16),
                        pltpu.VMEM((gc // 2, rows, 2 * L), BF16),
                        pltpu.VMEM((gc * L, n2), BF16),
                        pltpu.VMEM((gc * L, n2), BF16),
                        pltpu.VMEM((rows, gc * L), F32),
                        pltpu.VMEM((4, rows * gc // 4, L), F32)],
        compiler_params=_params("arbitrary"),
        name="ssm_chunk_conv",
    )(ut, kvec, kvec, rpt, fpt, ba, bb, ca, cb, lam_a, lam_b)


def _ssm_operators(a_re, a_im, log_dt, b_re, b_im, c_re, c_im):
    L = SSM_CHUNK
    hi = lax.Precision.HIGHEST
    a_re = a_re.astype(F32)
    a_im = a_im.astype(F32)
    dt = jnp.exp(log_dt.astype(F32))[:, None]
    mag = jnp.exp(a_re * dt)
    lam_re = mag * jnp.cos(a_im * dt)
    lam_im = mag * jnp.sin(a_im * dt)
    nr, ni = lam_re - 1.0, lam_im
    den = a_re * a_re + a_im * a_im
    coef_re = ((nr * a_re + ni * a_im) / den)[..., None]
    coef_im = ((ni * a_re - nr * a_im) / den)[..., None]
    b_re = b_re.astype(F32)
    b_im = b_im.astype(F32)
    bb_re = coef_re * b_re - coef_im * b_im
    bb_im = coef_re * b_im + coef_im * b_re
    cr = jnp.swapaxes(c_re.astype(F32), 1, 2)
    ci = jnp.swapaxes(c_im.astype(F32), 1, 2)

    k = jnp.arange(L + 1, dtype=F32)
    pmag = jnp.exp((a_re * dt)[..., None] * k)
    ang = (a_im * dt)[..., None] * k
    pw_re = pmag * jnp.cos(ang)
    pw_im = pmag * jnp.sin(ang)

    cb_re = cr[:, :, None, :] * bb_re[..., None] - ci[:, :, None, :] * bb_im[..., None]
    cb_im = cr[:, :, None, :] * bb_im[..., None] + ci[:, :, None, :] * bb_re[..., None]
    kvec = (jnp.einsum('gncd,gnk->gcdk', cb_re, pw_re[..., :L], precision=hi)
            - jnp.einsum('gncd,gnk->gcdk', cb_im, pw_im[..., :L], precision=hi))

    def lanes(re, im):
        return jnp.concatenate([jnp.swapaxes(re, 1, 2), jnp.swapaxes(im, 1, 2)], axis=-1)

    rpt = lanes(jnp.flip(pw_re[..., :L], axis=-1), jnp.flip(pw_im[..., :L], axis=-1))
    fpt = lanes(pw_re[..., 1:], pw_im[..., 1:])
    ba = lanes(bb_re, bb_re)
    bb = lanes(-bb_im, bb_im)
    ca = lanes(cr, -cr)
    cb = lanes(-ci, -ci)
    lam_a = jnp.concatenate([pw_re[..., L], pw_re[..., L]], axis=-1)[:, None, :]
    lam_b = jnp.concatenate([-pw_im[..., L], pw_im[..., L]], axis=-1)[:, None, :]
    return kvec, rpt, fpt, ba, bb, ca, cb, lam_a, lam_b


def _ssm_out_kernel(yt_ref, ut_ref, d_ref, wg_ref, bg_ref, wo_ref, x_ref, g_ref, b_ref, out_ref):
    gc = SSM_GROUP_CH
    groups, nq = yt_ref.shape[0], yt_ref.shape[1] // gc

    def channel_major(ref):
        return jnp.concatenate(
            [jnp.concatenate([ref[g, q * gc:(q + 1) * gc, :] for g in range(groups)], axis=0)
             for q in range(nq)], axis=1)

    y = channel_major(yt_ref) + d_ref[...] * channel_major(ut_ref)
    z = 0.5 * y * (1.0 + jnp.tanh(math.sqrt(2.0 / math.pi) * (y + 0.044715 * (y * y * y))))
    gate = _dot(wg_ref[...], z.astype(BF16)) + bg_ref[...]
    zs = z * (1.0 / (1.0 + jnp.exp(-gate)))
    f = _dot_tn(zs.astype(BF16), wo_ref[...])
    r = DEEPNORM_ALPHA * x_ref[...] + f
    out_ref[...] = _layer_norm(r, g_ref[...], b_ref[...])


def _ssm_out(yt, ut, d_skip, w_glu_t_bf16, b_glu, w_out_bf16, x2d, ln_g, ln_b, tn):
    tokens, d_model = x2d.shape
    groups = yt.shape[0]
    d_ssm = groups * SSM_GROUP_CH
    fixed = lambda i: (0, 0)
    chunked = pl.BlockSpec((groups, tn // SSM_CHUNK * SSM_GROUP_CH, SSM_CHUNK), lambda i: (0, i, 0))
    return pl.pallas_call(
        _ssm_out_kernel,
        out_shape=jax.ShapeDtypeStruct((tokens, d_model), F32),
        grid=(tokens // tn,),
        in_specs=[
            chunked,
            chunked,
            pl.BlockSpec((d_ssm, 1), fixed),
            pl.BlockSpec((d_ssm, d_ssm), fixed),
            pl.BlockSpec((d_ssm, 1), fixed),
            pl.BlockSpec((d_ssm, d_model), fixed),
            pl.BlockSpec((tn, d_model), lambda i: (i, 0)),
            pl.BlockSpec((1, d_model), fixed),
            pl.BlockSpec((1, d_model), fixed),
        ],
        out_specs=pl.BlockSpec((tn, d_model), lambda i: (i, 0)),
        compiler_params=_params("parallel"),
        name="ssm_glu_out_ln",
    )(yt, ut, d_skip.reshape(-1, 1), w_glu_t_bf16, b_glu.reshape(-1, 1), w_out_bf16, x2d,
      ln_g.reshape(1, -1), ln_b.reshape(1, -1))


def _rope_tables(seq):
    pos = jnp.arange(seq, dtype=F32)
    inv_freq = ROPE_THETA ** (-jnp.arange(0, HEAD_DIM, 2, dtype=F32) / HEAD_DIM)
    ang = pos[:, None] * inv_freq[None, :]
    cos = jnp.cos(ang)
    sin = jnp.sin(ang)
    return jnp.concatenate([cos, cos], axis=1), jnp.concatenate([-sin, sin], axis=1)


def _row_tile(seq, want):
    tm = min(want, seq)
    assert seq % tm == 0
    return tm


def kernel(x, attn_w_in, attn_w_out, ssm_w_in, ssm_a_re, ssm_a_im, ssm_log_dt, ssm_b_re, ssm_b_im,
           ssm_c_re, ssm_c_im, ssm_d, ssm_w_glu, ssm_b_glu, ssm_w_out, ffn_w_up, ffn_conv_w,
           ffn_conv_b, ffn_w_down, ln_g, ln_b):
    batch, seq, d_model = x.shape
    assert seq % DILATION_PAIRS[-1][0] == 0 and seq % SSM_CHUNK == 0
    assert all(w // d == ATTN_BLOCK for w, d in DILATION_PAIRS)
    h = x.reshape(batch * seq, d_model)
    ffn_chunk = 256

    def ffn(h, i):
        return _conv_ffn(h, ffn_w_up[i].astype(BF16), ffn_conv_w[i], ffn_conv_b[i],
                         ffn_w_down[i].astype(BF16), ln_g[i, 1], ln_b[i, 1], seq,
                         _row_tile(seq, 1024), ffn_chunk)

    cos_tab, sin_tab = _rope_tables(seq)
    w_in = attn_w_in[0].astype(BF16)
    os, lses = [], []
    for g, (_, dilation) in enumerate(DILATION_PAIRS):
        qkv_g = _qkv_rope(h, w_in, cos_tab, sin_tab, g, dilation, batch, seq, _row_tile(seq, 1024))
        o, lse = _dilated_attention_group(qkv_g, g, 1024)
        os.append(o)
        lses.append(lse)
    h = _merge_out(os, lses, h, attn_w_out[0].astype(BF16), ln_g[0, 0], ln_b[0, 0], seq,
                   _row_tile(seq, 512))
    h = ffn(h, 0)

    ssm_ops = _ssm_operators(
        ssm_a_re[0], ssm_a_im[0], ssm_log_dt[0], ssm_b_re[0], ssm_b_im[0], ssm_c_re[0], ssm_c_im[0])
    ut = _ssm_in(h, ssm_w_in[0].T.astype(BF16), _row_tile(seq, 1024))
    yt = _ssm_core(ut, *ssm_ops, seq // SSM_CHUNK)
    h = _ssm_out(yt, ut, ssm_d[0], ssm_w_glu[0].T.astype(BF16), ssm_b_glu[0],
                 ssm_w_out[0].astype(BF16), h, ln_g[1, 0], ln_b[1, 0], _row_tile(seq, 512))
    h = ffn(h, 1)
    return h.reshape(batch, seq, d_model)
```

```python
import functools
import math

import jax
import jax.numpy as jnp
from jax import lax
from jax.experimental import pallas as pl
from jax.experimental.pallas import tpu as pltpu

F32 = jnp.float32
BF16 = jnp.bfloat16

DEPTH = 2
DILATION_PAIRS = ((128, 1), (512, 4), (2048, 16))
N_GROUPS = len(DILATION_PAIRS)
HEADS = 8
HEAD_DIM = 128
ROPE_THETA = 10000.0
SSM_GROUP_CH = 16
SSM_STATE = 64
CONV_WIDTH = 3
DEEPNORM_ALPHA = (2.0 * DEPTH) ** 0.25
LN_EPS = 1e-5
NEG_INF = -1e30
LOG2E = math.log2(math.e)
LN2 = math.log(2.0)

LANES = 128
SUBLANES = 8
VMEM_LIMIT_BYTES = 56 * 1024 * 1024

ATTN_BLOCK = 128
SSM_CHUNK = 128


def _params(*semantics):
    return pltpu.CompilerParams(dimension_semantics=semantics,
                                vmem_limit_bytes=VMEM_LIMIT_BYTES)


def _layer_norm(r, g, b):
    mu = jnp.mean(r, axis=-1, keepdims=True)
    d = r - mu
    var = jnp.mean(d * d, axis=-1, keepdims=True)
    return d * lax.rsqrt(var + LN_EPS) * g + b


def _dot(a, b):
    return jnp.dot(a, b, preferred_element_type=F32)


def _dot_nt(a, b):
    return lax.dot_general(a, b, (((1,), (1,)), ((), ())), preferred_element_type=F32)


def _dot_tn(a, b):
    return lax.dot_general(a, b, (((0,), (0,)), ((), ())), preferred_element_type=F32)


def _residue_rows(r, n, d):
    return pl.ds(r, n, stride=d) if d > 1 else slice(None)


def _qkv_kernel(*refs, d):
    n_slabs = len(refs) - 8
    x_refs = refs[:n_slabs]
    w_ref, cos_ref, sin_ref, o_ref, xb_ref, cs_ref, sn_ref, tmp_ref = refs[n_slabs:]
    tm = cos_ref.shape[0]
    n = tm // d
    hw = HEADS * HEAD_DIM

    def residues(src_ref):
        if d % 8:
            return [src_ref[_residue_rows(r, n, d), :] for r in range(d)]
        for a in range(4):
            tmp_ref[a] = src_ref[pl.ds(a, tm // 4, stride=4), :]
        return [tmp_ref[r % 4, pl.ds(r // 4, n, stride=d // 4), :] for r in range(d)]

    for c, x_ref in enumerate(x_refs):
        for r, piece in enumerate(residues(x_ref)):
            xb_ref[r * n:(r + 1) * n, c * LANES:(c + 1) * LANES] = piece.astype(BF16)
    for src_ref, dst_ref in ((cos_ref, cs_ref), (sin_ref, sn_ref)):
        for r, piece in enumerate(residues(src_ref)):
            dst_ref[r * n:(r + 1) * n, :] = piece
    xb = xb_ref[...]

    for kind in range(3):
        acc = _dot(xb, w_ref[:, kind * hw:(kind + 1) * hw])
        if kind == 2:
            for r in range(d):
                o_ref[r, :, kind * hw:(kind + 1) * hw] = acc[r * n:(r + 1) * n, :].astype(BF16)
            continue
        scale = HEAD_DIM ** -0.5 * LOG2E if kind == 0 else 1.0
        cos = cs_ref[...] * scale
        sin = sn_ref[...] * scale
        for h in range(HEADS):
            t = acc[:, h * HEAD_DIM:(h + 1) * HEAD_DIM]
            rot = (t * cos + pltpu.roll(t, HEAD_DIM // 2, axis=1) * sin).astype(BF16)
            lanes = slice(kind * hw + h * HEAD_DIM, kind * hw + (h + 1) * HEAD_DIM)
            for r in range(d):
                o_ref[r, :, lanes] = rot[r * n:(r + 1) * n, :]


def _qkv_rope(x2d, w_in_bf16, cos_tab, sin_tab, g, d, batch, seq, tm):
    d_model = x2d.shape[1]
    hw = HEADS * HEAD_DIM
    seq_tiles = seq // tm
    n = tm // d
    return pl.pallas_call(
        functools.partial(_qkv_kernel, d=d),
        out_shape=jax.ShapeDtypeStruct((batch, d, seq // d, 3 * hw), BF16),
        grid=(batch * seq_tiles,),
        in_specs=[pl.BlockSpec((tm, LANES), lambda i, c=c: (i, c))
                  for c in range(d_model // LANES)] + [
            pl.BlockSpec((d_model, 3 * hw), lambda i: (0, g)),
            pl.BlockSpec((tm, HEAD_DIM), lambda i: (i % seq_tiles, 0)),
            pl.BlockSpec((tm, HEAD_DIM), lambda i: (i % seq_tiles, 0)),
        ],
        out_specs=pl.BlockSpec((None, d, n, 3 * hw),
                               lambda i: (i // seq_tiles, 0, i % seq_tiles, 0)),
        scratch_shapes=[pltpu.VMEM((tm, d_model), BF16),
                        pltpu.VMEM((tm, HEAD_DIM), F32),
                        pltpu.VMEM((tm, HEAD_DIM), F32),
                        pltpu.VMEM((4, tm // 4, LANES), F32)],
        compiler_params=_params("parallel"),
        name=f"qkv_rope_g{g}",
    )(*([x2d] * (d_model // LANES)), w_in_bf16, cos_tab, sin_tab)


def _attn_kernel(q_ref, k_ref, v_ref, kp_ref, vp_ref, o_ref, lse_ref, *, nblk):
    m = pl.program_id(2)
    blk_rows = ATTN_BLOCK
    row = lax.broadcasted_iota(jnp.int32, (blk_rows, 2 * blk_rows), 0)
    col = lax.broadcasted_iota(jnp.int32, (blk_rows, 2 * blk_rows), 1)
    band = jnp.logical_and(col >= row, col <= row + blk_rows)
    first_band = jnp.logical_and(band, jnp.logical_or(col >= blk_rows, m > 0))
    lane = lax.broadcasted_iota(jnp.int32, (blk_rows, LANES), 1)
    ones = jnp.ones((2 * blk_rows, HEAD_DIM), BF16)

    def block(ri, q_rows, keys, vals, mask):
        mx_tile = jnp.zeros((blk_rows, LANES), F32)
        l_tile = jnp.ones((blk_rows, LANES), F32)
        for h in range(HEADS):
            cols = slice(h * HEAD_DIM, (h + 1) * HEAD_DIM)
            s = jnp.where(mask, _dot_nt(q_ref[ri, q_rows, cols], keys(cols)), NEG_INF)
            mx = jnp.max(jnp.maximum(s[:, :blk_rows], s[:, blk_rows:]), axis=1, keepdims=True)
            p = jnp.exp2(s - mx).astype(BF16)
            oe = _dot(p, jnp.concatenate([vals(cols), ones], axis=1))
            l = oe[:, HEAD_DIM:]
            o_ref[ri, q_rows, cols] = (oe[:, :HEAD_DIM] / l).astype(BF16)
            mx_tile = jnp.where(lane == h, mx, mx_tile)
            l_tile = jnp.where(lane == h, l, l_tile)
        lse_ref[ri, q_rows, :] = mx_tile * LN2 + jnp.log(l_tile)

    head = slice(0, blk_rows)
    for ri in range(q_ref.shape[0]):
        block(ri, head,
              lambda cols: jnp.concatenate([kp_ref[ri, :, cols], k_ref[ri, head, cols]], axis=0),
              lambda cols: jnp.concatenate([vp_ref[ri, :, cols], v_ref[ri, head, cols]], axis=0),
              first_band)
        for blk in range(1, nblk):
            q_rows = slice(blk * blk_rows, (blk + 1) * blk_rows)
            kv_rows = slice((blk - 1) * blk_rows, (blk + 1) * blk_rows)
            block(ri, q_rows, lambda cols: k_ref[ri, kv_rows, cols],
                  lambda cols: v_ref[ri, kv_rows, cols], band)


def _dilated_attention_group(qkv_g, g, rows):
    batch, d, sub_len, _ = qkv_g.shape
    hw = HEADS * HEAD_DIM
    res = max(1, min(d, rows // sub_len))
    rows = min(rows, sub_len)
    nblk = rows // ATTN_BLOCK

    def main_spec(kind):
        return pl.BlockSpec((None, res, rows, hw), lambda b, r, m: (b, r, m, kind))

    def prev_spec(kind):
        return pl.BlockSpec((None, res, ATTN_BLOCK, hw),
                            lambda b, r, m: (b, r, jnp.maximum(m * nblk - 1, 0), kind))

    return pl.pallas_call(
        functools.partial(_attn_kernel, nblk=nblk),
        out_shape=(jax.ShapeDtypeStruct((batch, d, sub_len, hw), BF16),
                   jax.ShapeDtypeStruct((batch, d, sub_len, LANES), F32)),
        grid=(batch, d // res, sub_len // rows),
        in_specs=[main_spec(0), main_spec(1), main_spec(2), prev_spec(1), prev_spec(2)],
        out_specs=(pl.BlockSpec((None, res, rows, hw), lambda b, r, m: (b, r, m, 0)),
                   pl.BlockSpec((None, res, rows, LANES), lambda b, r, m: (b, r, m, 0))),
        compiler_params=_params("parallel", "parallel", "arbitrary"),
        name=f"dilated_attn_g{g}",
    )(qkv_g, qkv_g, qkv_g, qkv_g, qkv_g)


def _merge_out_kernel(o0_ref, o1_ref, o2_ref, l0_ref, l1_ref, l2_ref, e_ref, x_ref, w_ref,
                      g_ref, b_ref, out_ref, on_ref, ln_ref):
    tm = x_ref.shape[0]
    outs, lses = [], []
    for gi, (o_ref, l_ref) in enumerate(((o0_ref, l0_ref), (o1_ref, l1_ref), (o2_ref, l2_ref))):
        d = o_ref.shape[0]
        n = tm // d
        if d == 1:
            outs.append(o_ref[0].astype(F32))
            lses.append(l_ref[0])
            continue
        for r in range(d):
            rows = _residue_rows(r, n, d)
            o_r = o_ref[r].astype(F32)
            for c in range(on_ref.shape[1]):
                on_ref[gi, c, rows, :] = o_r[:, c * LANES:(c + 1) * LANES]
            ln_ref[gi, rows, :] = l_ref[r]
        outs.append(jnp.concatenate([on_ref[gi, c] for c in range(on_ref.shape[1])], axis=1))
        lses.append(ln_ref[gi])
    mx = jnp.maximum(jnp.maximum(lses[0], lses[1]), lses[2])
    ws = [jnp.exp(l - mx) for l in lses]
    inv = 1.0 / (ws[0] + ws[1] + ws[2])
    e = e_ref[...]
    merged = None
    for w, o in zip(ws, outs):
        w = w * inv
        w_hi = w.astype(BF16)
        w_lo = (w - w_hi.astype(F32)).astype(BF16)
        w_full = _dot(jnp.concatenate([w_hi, w_lo], axis=1), e)
        term = w_full * o
        merged = term if merged is None else merged + term
    y = _dot(merged.astype(BF16), w_ref[...])
    r = DEEPNORM_ALPHA * x_ref[...] + y
    out_ref[...] = _layer_norm(r, g_ref[...], b_ref[...])


def _merge_out(os, lses, x2d, w_out_bf16, ln_g, ln_b, seq, tm):
    tokens, d_model = x2d.shape
    hw = HEADS * HEAD_DIM
    seq_tiles = seq // tm
    expand = (jnp.arange(2 * LANES)[:, None] % LANES == (jnp.arange(hw)[None, :] // HEAD_DIM)).astype(BF16)
    row = lambda i: (i, 0)
    fixed = lambda i: (0, 0)

    def group_spec(arr):
        d, width = arr.shape[1], arr.shape[3]
        return pl.BlockSpec((None, d, tm // d, width),
                            lambda i: (i // seq_tiles, 0, i % seq_tiles, 0))

    return pl.pallas_call(
        _merge_out_kernel,
        out_shape=jax.ShapeDtypeStruct((tokens, d_model), F32),
        grid=(tokens // tm,),
        in_specs=[group_spec(a) for a in os] + [group_spec(a) for a in lses] + [
            pl.BlockSpec((2 * LANES, hw), fixed),
            pl.BlockSpec((tm, d_model), row),
            pl.BlockSpec((hw, d_model), fixed),
            pl.BlockSpec((1, d_model), fixed),
            pl.BlockSpec((1, d_model), fixed),
        ],
        out_specs=pl.BlockSpec((tm, d_model), row),
        scratch_shapes=[pltpu.VMEM((N_GROUPS, hw // LANES, tm, LANES), F32),
                        pltpu.VMEM((N_GROUPS, tm, LANES), F32)],
        compiler_params=_params("parallel"),
        name="attn_merge_out_ln",
    )(*os, *lses, expand, x2d, w_out_bf16, ln_g.reshape(1, -1), ln_b.reshape(1, -1))


def _ffn_kernel(x_ref, halo_ref, wup_ref, cw_ref, cb_ref, wdown_ref, g_ref, b_ref, out_ref,
                xe_ref, h_ref, *, seq_tiles, d_ff, chunk):
    i = pl.program_id(0)
    x = x_ref[...]
    first = (i % seq_tiles) == 0
    xe_ref[0:SUBLANES, :] = jnp.where(first, 0.0, halo_ref[...]).astype(BF16)
    xe_ref[SUBLANES:, :] = x.astype(BF16)
    xe = xe_ref[...]
    for c in range(d_ff // chunk):
        gcols = slice(c * chunk, (c + 1) * chunk)
        vcols = slice(d_ff + c * chunk, d_ff + (c + 1) * chunk)
        gate = _dot(xe, wup_ref[:, gcols])
        val = _dot(xe[SUBLANES:, :], wup_ref[:, vcols])
        conv = (cb_ref[:, gcols]
                + cw_ref[0:1, gcols] * pltpu.roll(gate, 2, axis=0)[SUBLANES:, :]
                + cw_ref[1:2, gcols] * pltpu.roll(gate, 1, axis=0)[SUBLANES:, :]
                + cw_ref[2:3, gcols] * gate[SUBLANES:, :])
        act = conv * (1.0 / (1.0 + jnp.exp(-conv)))
        h_ref[:, gcols] = (act * val).astype(BF16)
    f = _dot(h_ref[...], wdown_ref[...])
    r = DEEPNORM_ALPHA * x + f
    out_ref[...] = _layer_norm(r, g_ref[...], b_ref[...])


def _conv_ffn(x2d, w_up_bf16, conv_w, conv_b, w_down_bf16, ln_g, ln_b, seq, tm, chunk):
    tokens, d_model = x2d.shape
    d_ff = w_down_bf16.shape[0]
    seq_tiles = seq // tm
    halo_blocks = tm // SUBLANES
    fixed = lambda i: (0, 0)
    single = dict(pipeline_mode=pl.Buffered(1))
    return pl.pallas_call(
        functools.partial(_ffn_kernel, seq_tiles=seq_tiles, d_ff=d_ff, chunk=chunk),
        out_shape=jax.ShapeDtypeStruct((tokens, d_model), F32),
        grid=(tokens // tm,),
        in_specs=[
            pl.BlockSpec((tm, d_model), lambda i: (i, 0)),
            pl.BlockSpec((SUBLANES, d_model), lambda i: (jnp.maximum(i * halo_blocks - 1, 0), 0)),
            pl.BlockSpec((d_model, 2 * d_ff), fixed, **single),
            pl.BlockSpec((CONV_WIDTH, d_ff), fixed),
            pl.BlockSpec((1, d_ff), fixed),
            pl.BlockSpec((d_ff, d_model), fixed, **single),
            pl.BlockSpec((1, d_model), fixed),
            pl.BlockSpec((1, d_model), fixed),
        ],
        out_specs=pl.BlockSpec((tm, d_model), lambda i: (i, 0)),
        scratch_shapes=[pltpu.VMEM((tm + SUBLANES, d_model), BF16),
                        pltpu.VMEM((tm, d_ff), BF16)],
        compiler_params=_params("parallel"),
        name="conv_ffn_ln",
    )(x2d, x2d, w_up_bf16, conv_w, conv_b.reshape(1, -1), w_down_bf16,
      ln_g.reshape(1, -1), ln_b.reshape(1, -1))


def _ssm_in_kernel(x_ref, wt_ref, ut_ref):
    gc = SSM_GROUP_CH
    u = _dot_nt(wt_ref[...], x_ref[...].astype(BF16))
    for g in range(ut_ref.shape[0]):
        for q in range(ut_ref.shape[1] // gc):
            ut_ref[g, q * gc:(q + 1) * gc, :] = u[g * gc:(g + 1) * gc, q * SSM_CHUNK:(q + 1) * SSM_CHUNK]


def _ssm_in(x2d, w_in_t_bf16, tn):
    tokens, d_model = x2d.shape
    d_ssm = w_in_t_bf16.shape[0]
    gc = SSM_GROUP_CH
    groups = d_ssm // gc
    return pl.pallas_call(
        _ssm_in_kernel,
        out_shape=jax.ShapeDtypeStruct((groups, tokens // SSM_CHUNK * gc, SSM_CHUNK), F32),
        grid=(tokens // tn,),
        in_specs=[pl.BlockSpec((tn, d_model), lambda i: (i, 0)),
                  pl.BlockSpec((d_ssm, d_model), lambda i: (0, 0))],
        out_specs=pl.BlockSpec((groups, tn // SSM_CHUNK * gc, SSM_CHUNK), lambda i: (0, i, 0)),
        compiler_params=_params("parallel"),
        name="ssm_in_proj",
    )(x2d, w_in_t_bf16)


def _ssm_core_kernel(ut_ref, kv0_ref, kvn_ref, rpt_ref, fpt_ref, ba_ref, bb_ref, ca_ref, cb_ref,
                     la_ref, lb_ref, yt_ref, toep0_ref, toep1_ref, lhs_ref, pm_ref, cmt_ref, yacc_ref,
                     tmp_ref, *, chunks_per_seq):
    gc = SSM_GROUP_CH
    L = SSM_CHUNK
    rows = ut_ref.shape[0] // gc
    n2 = 2 * SSM_STATE
    pair = 2 * L
    steps = gc // 2
    g = pl.program_id(0)
    slot = g % 2

    srow = lax.broadcasted_iota(jnp.int32, (L, L), 0)
    tcol = lax.broadcasted_iota(jnp.int32, (L, L), 1)
    causal = tcol >= srow

    def build_pair(kv_ref, i, dst_ref):
        for cc in range(2):
            c = 2 * i + cc
            for cp in range(gc):
                kv = kv_ref[c, pl.ds(cp, 1), :]
                blk = pltpu.roll(jnp.broadcast_to(kv, (L, L)), 0, 1, stride=1, stride_axis=0)
                blk = jnp.where(causal, blk, 0.0)
                dst_ref[pl.ds(pl.multiple_of(c * L, L), L), cp * L:(cp + 1) * L] = blk.astype(BF16)

    @pl.when(g == 0)
    def _():
        def first(i, carry):
            build_pair(kv0_ref, i, toep0_ref)
            return carry
        lax.fori_loop(0, steps, first, 0)

    quarter = rows * gc // 4
    for a in range(4):
        tmp_ref[a] = ut_ref[pl.ds(a, quarter, stride=4), :]
    for c in range(gc):
        piece = tmp_ref[c % 4, pl.ds(c // 4, rows, stride=4), :].astype(BF16)
        lhs_ref[c // 2, :, (c % 2) * L:(c % 2 + 1) * L] = piece

    rpt = rpt_ref[...]
    rpt_sw = pltpu.roll(rpt, SSM_STATE, axis=1)
    fpt = fpt_ref[...]
    fpt_sw = pltpu.roll(fpt, SSM_STATE, axis=1)
    for c in range(gc):
        pm_ref[c * L:(c + 1) * L, :] = (rpt * ba_ref[c:c + 1, :] + rpt_sw * bb_ref[c:c + 1, :]).astype(BF16)
        cmt_ref[c * L:(c + 1) * L, :] = (fpt * ca_ref[c:c + 1, :] + fpt_sw * cb_ref[c:c + 1, :]).astype(BF16)

    st = None
    for i in range(steps):
        part = _dot(lhs_ref[i], pm_ref[i * pair:(i + 1) * pair, :])
        st = part if st is None else st + part

    a = la_ref[...]
    b = lb_ref[...]
    jrow = lax.broadcasted_iota(jnp.int32, (rows, n2), 0) % chunks_per_seq
    sh = 1
    while sh < chunks_per_seq:
        prev = jnp.where(jrow >= sh, pltpu.roll(st, sh, axis=0), 0.0)
        st = st + prev * a + pltpu.roll(prev, SSM_STATE, axis=1) * b
        a, b = a * a - b * b, 2.0 * a * b
        sh *= 2
    carried = jnp.where(jrow >= 1, pltpu.roll(st, 1, axis=0), 0.0)
    yacc_ref[...] = _dot_nt(carried.astype(BF16), cmt_ref[...])

    def pipeline(cur_ref, nxt_ref):
        def step(i, carry):
            build_pair(kvn_ref, i, nxt_ref)
            rws = pl.ds(pl.multiple_of(i * pair, pair), pair)
            yacc_ref[...] += _dot(lhs_ref[i], cur_ref[rws, :])
            return carry
        lax.fori_loop(0, steps, step, 0)

    @pl.when(slot == 0)
    def _():
        pipeline(toep0_ref, toep1_ref)

    @pl.when(slot == 1)
    def _():
        pipeline(toep1_ref, toep0_ref)

    for cp in range(gc):
        tmp_ref[cp % 4, pl.ds(cp // 4, rows, stride=4), :] = yacc_ref[:, cp * L:(cp + 1) * L]
    for a4 in range(4):
        yt_ref[pl.ds(a4, quarter, stride=4), :] = tmp_ref[a4]


def _ssm_core(ut, kvec, rpt, fpt, ba, bb, ca, cb, lam_a, lam_b, chunks_per_seq):
    groups, group_rows, L = ut.shape
    gc, n2 = SSM_GROUP_CH, 2 * SSM_STATE
    rows = group_rows // gc
    per_group = lambda g: (g, 0, 0)
    return pl.pallas_call(
        functools.partial(_ssm_core_kernel, chunks_per_seq=chunks_per_seq),
        out_shape=jax.ShapeDtypeStruct((groups, group_rows, L), F32),
        grid=(groups,),
        in_specs=[
            pl.BlockSpec((None, group_rows, L), per_group),
            pl.BlockSpec((None, gc, gc, L), lambda g: (0, 0, 0, 0)),
            pl.BlockSpec((None, gc, gc, L), lambda g: (jnp.minimum(g + 1, groups - 1), 0, 0, 0)),
            pl.BlockSpec((None, L, n2), per_group),
            pl.BlockSpec((None, L, n2), per_group),
            pl.BlockSpec((None, gc, n2), per_group),
            pl.BlockSpec((None, gc, n2), per_group),
            pl.BlockSpec((None, gc, n2), per_group),
            pl.BlockSpec((None, gc, n2), per_group),
            pl.BlockSpec((None, 1, n2), per_group),
            pl.BlockSpec((None, 1, n2), per_group),
        ],
        out_specs=pl.BlockSpec((None, group_rows, L), per_group),
        scratch_shapes=[pltpu.VMEM((gc * L, gc * L), BF16),
                        pltpu.VMEM((gc * L, gc * L), BF16),
                        pltpu.VMEM((gc // 2, rows, 2 * L), BF16),
                        pltpu.VMEM((gc * L, n2), BF16),
                        pltpu.VMEM((gc * L, n2), BF16),
                        pltpu.VMEM((rows, gc * L), F32),
                        pltpu.VMEM((4, rows * gc // 4, L), F32)],
        compiler_params=_params("arbitrary"),
        name="ssm_chunk_conv",
    )(ut, kvec, kvec, rpt, fpt, ba, bb, ca, cb, lam_a, lam_b)


def _ssm_operators(a_re, a_im, log_dt, b_re, b_im, c_re, c_im):
    L = SSM_CHUNK
    hi = lax.Precision.HIGHEST
    a_re = a_re.astype(F32)
    a_im = a_im.astype(F32)
    dt = jnp.exp(log_dt.astype(F32))[:, None]
    mag = jnp.exp(a_re * dt)
    lam_re = mag * jnp.cos(a_im * dt)
    lam_im = mag * jnp.sin(a_im * dt)
    nr, ni = lam_re - 1.0, lam_im
    den = a_re * a_re + a_im * a_im
    coef_re = ((nr * a_re + ni * a_im) / den)[..., None]
    coef_im = ((ni * a_re - nr * a_im) / den)[..., None]
    b_re = b_re.astype(F32)
    b_im = b_im.astype(F32)
    bb_re = coef_re * b_re - coef_im * b_im
    bb_im = coef_re * b_im + coef_im * b_re
    cr = jnp.swapaxes(c_re.astype(F32), 1, 2)
    ci = jnp.swapaxes(c_im.astype(F32), 1, 2)

    k = jnp.arange(L + 1, dtype=F32)
    pmag = jnp.exp((a_re * dt)[..., None] * k)
    ang = (a_im * dt)[..., None] * k
    pw_re = pmag * jnp.cos(ang)
    pw_im = pmag * jnp.sin(ang)

    cb_re = cr[:, :, None, :] * bb_re[..., None] - ci[:, :, None, :] * bb_im[..., None]
    cb_im = cr[:, :, None, :] * bb_im[..., None] + ci[:, :, None, :] * bb_re[..., None]
    kvec = (jnp.einsum('gncd,gnk->gcdk', cb_re, pw_re[..., :L], precision=hi)
            - jnp.einsum('gncd,gnk->gcdk', cb_im, pw_im[..., :L], precision=hi))

    def lanes(re, im):
        return jnp.concatenate([jnp.swapaxes(re, 1, 2), jnp.swapaxes(im, 1, 2)], axis=-1)

    rpt = lanes(jnp.flip(pw_re[..., :L], axis=-1), jnp.flip(pw_im[..., :L], axis=-1))
    fpt = lanes(pw_re[..., 1:], pw_im[..., 1:])
    ba = lanes(bb_re, bb_re)
    bb = lanes(-bb_im, bb_im)
    ca = lanes(cr, -cr)
    cb = lanes(-ci, -ci)
    lam_a = jnp.concatenate([pw_re[..., L], pw_re[..., L]], axis=-1)[:, None, :]
    lam_b = jnp.concatenate([-pw_im[..., L], pw_im[..., L]], axis=-1)[:, None, :]
    return kvec, rpt, fpt, ba, bb, ca, cb, lam_a, lam_b


def _ssm_out_kernel(yt_ref, ut_ref, d_ref, wg_ref, bg_ref, wo_ref, x_ref, g_ref, b_ref, out_ref, *, sub):
    gc = SSM_GROUP_CH
    groups, nq = yt_ref.shape[0], yt_ref.shape[1] // gc
    qs = sub // SSM_CHUNK
    n_sub = nq // qs

    def channel_major(ref, q0):
        return jnp.concatenate(
            [jnp.concatenate([ref[g, q * gc:(q + 1) * gc, :] for g in range(groups)], axis=0)
             for q in range(q0, q0 + qs)], axis=1)

    def gelu_gate(s):
        y = channel_major(yt_ref, s * qs) + d_ref[...] * channel_major(ut_ref, s * qs)
        z = 0.5 * y * (1.0 + jnp.tanh(math.sqrt(2.0 / math.pi) * (y + 0.044715 * (y * y * y))))
        return z, _dot(wg_ref[...], z.astype(BF16)) + bg_ref[...]

    def glu_out(s, z, gate):
        rows = slice(s * sub, (s + 1) * sub)
        zs = z * (1.0 / (1.0 + jnp.exp(-gate)))
        f = _dot_tn(zs.astype(BF16), wo_ref[...])
        r = DEEPNORM_ALPHA * x_ref[rows, :] + f
        out_ref[rows, :] = _layer_norm(r, g_ref[...], b_ref[...])

    pending = gelu_gate(0)
    for s in range(n_sub):
        upcoming = gelu_gate(s + 1) if s + 1 < n_sub else None
        glu_out(s, *pending)
        pending = upcoming


def _ssm_out(yt, ut, d_skip, w_glu_t_bf16, b_glu, w_out_bf16, x2d, ln_g, ln_b, tn, sub):
    tokens, d_model = x2d.shape
    groups = yt.shape[0]
    d_ssm = groups * SSM_GROUP_CH
    fixed = lambda i: (0, 0)
    chunked = pl.BlockSpec((groups, tn // SSM_CHUNK * SSM_GROUP_CH, SSM_CHUNK), lambda i: (0, i, 0))
    return pl.pallas_call(
        functools.partial(_ssm_out_kernel, sub=sub),
        out_shape=jax.ShapeDtypeStruct((tokens, d_model), F32),
        grid=(tokens // tn,),
        in_specs=[
            chunked,
            chunked,
            pl.BlockSpec((d_ssm, 1), fixed),
            pl.BlockSpec((d_ssm, d_ssm), fixed),
            pl.BlockSpec((d_ssm, 1), fixed),
            pl.BlockSpec((d_ssm, d_model), fixed),
            pl.BlockSpec((tn, d_model), lambda i: (i, 0)),
            pl.BlockSpec((1, d_model), fixed),
            pl.BlockSpec((1, d_model), fixed),
        ],
        out_specs=pl.BlockSpec((tn, d_model), lambda i: (i, 0)),
        compiler_params=_params("parallel"),
        name="ssm_glu_out_ln",
    )(yt, ut, d_skip.reshape(-1, 1), w_glu_t_bf16, b_glu.reshape(-1, 1), w_out_bf16, x2d,
      ln_g.reshape(1, -1), ln_b.reshape(1, -1))


def _rope_tables(seq):
    pos = jnp.arange(seq, dtype=F32)
    inv_freq = ROPE_THETA ** (-jnp.arange(0, HEAD_DIM, 2, dtype=F32) / HEAD_DIM)
    ang = pos[:, None] * inv_freq[None, :]
    cos = jnp.cos(ang)
    sin = jnp.sin(ang)
    return jnp.concatenate([cos, cos], axis=1), jnp.concatenate([-sin, sin], axis=1)


def _row_tile(seq, want):
    tm = min(want, seq)
    assert seq % tm == 0
    return tm


def kernel(x, attn_w_in, attn_w_out, ssm_w_in, ssm_a_re, ssm_a_im, ssm_log_dt, ssm_b_re, ssm_b_im,
           ssm_c_re, ssm_c_im, ssm_d, ssm_w_glu, ssm_b_glu, ssm_w_out, ffn_w_up, ffn_conv_w,
           ffn_conv_b, ffn_w_down, ln_g, ln_b):
    batch, seq, d_model = x.shape
    assert seq % DILATION_PAIRS[-1][0] == 0 and seq % SSM_CHUNK == 0
    assert all(w // d == ATTN_BLOCK for w, d in DILATION_PAIRS)
    h = x.reshape(batch * seq, d_model)
    ffn_chunk = 256

    def ffn(h, i):
        return _conv_ffn(h, ffn_w_up[i].astype(BF16), ffn_conv_w[i], ffn_conv_b[i],
                         ffn_w_down[i].astype(BF16), ln_g[i, 1], ln_b[i, 1], seq,
                         _row_tile(seq, 1024), ffn_chunk)

    cos_tab, sin_tab = _rope_tables(seq)
    w_in = attn_w_in[0].astype(BF16)
    os, lses = [], []
    for g, (_, dilation) in enumerate(DILATION_PAIRS):
        qkv_g = _qkv_rope(h, w_in, cos_tab, sin_tab, g, dilation, batch, seq, _row_tile(seq, 1024))
        o, lse = _dilated_attention_group(qkv_g, g, 1024)
        os.append(o)
        lses.append(lse)
    h = _merge_out(os, lses, h, attn_w_out[0].astype(BF16), ln_g[0, 0], ln_b[0, 0], seq,
                   _row_tile(seq, 512))
    h = ffn(h, 0)

    ssm_ops = _ssm_operators(
        ssm_a_re[0], ssm_a_im[0], ssm_log_dt[0], ssm_b_re[0], ssm_b_im[0], ssm_c_re[0], ssm_c_im[0])
    ut = _ssm_in(h, ssm_w_in[0].T.astype(BF16), _row_tile(seq, 1024))
    yt = _ssm_core(ut, *ssm_ops, seq // SSM_CHUNK)
    h = _ssm_out(yt, ut, ssm_d[0], ssm_w_glu[0].T.astype(BF16), ssm_b_glu[0],
                 ssm_w_out[0].astype(BF16), h, ln_g[1, 0], ln_b[1, 0], _row_tile(seq, 1024), 512)
    h = ffn(h, 1)
    return h.reshape(batch, seq, d_model)
```

```python
import functools
import math

import jax
import jax.numpy as jnp
from jax import lax
from jax.experimental import pallas as pl
from jax.experimental.pallas import tpu as pltpu

F32 = jnp.float32
BF16 = jnp.bfloat16

DEPTH = 2
DILATION_PAIRS = ((128, 1), (512, 4), (2048, 16))
N_GROUPS = len(DILATION_PAIRS)
HEADS = 8
HEAD_DIM = 128
ROPE_THETA = 10000.0
SSM_GROUP_CH = 16
SSM_STATE = 64
CONV_WIDTH = 3
DEEPNORM_ALPHA = (2.0 * DEPTH) ** 0.25
LN_EPS = 1e-5
NEG_INF = -1e30
LOG2E = math.log2(math.e)
LN2 = math.log(2.0)

LANES = 128
SUBLANES = 8
VMEM_LIMIT_BYTES = 56 * 1024 * 1024

ATTN_BLOCK = 128
ROPE_BLOCK = 128
SSM_CHUNK = 128


def _params(*semantics):
    return pltpu.CompilerParams(dimension_semantics=semantics,
                                vmem_limit_bytes=VMEM_LIMIT_BYTES)


def _layer_norm(r, g, b):
    mu = jnp.mean(r, axis=-1, keepdims=True)
    d = r - mu
    var = jnp.mean(d * d, axis=-1, keepdims=True)
    return d * lax.rsqrt(var + LN_EPS) * g + b


def _dot(a, b):
    return jnp.dot(a, b, preferred_element_type=F32)


def _dot_nt(a, b):
    return lax.dot_general(a, b, (((1,), (1,)), ((), ())), preferred_element_type=F32)


def _dot_tn(a, b):
    return lax.dot_general(a, b, (((0,), (0,)), ((), ())), preferred_element_type=F32)


def _residue_rows(r, n, d):
    return pl.ds(r, n, stride=d) if d > 1 else slice(None)


def _qkv_kernel(*refs, d):
    n_slabs = len(refs) - 14
    x_refs = refs[:n_slabs]
    (w_ref, ca_ref, sa_ref, cb_ref, sb_ref, cbs_ref, sbs_ref, o_ref,
     xb_ref, cos_ref, sin_ref, cs_ref, sn_ref, tmp_ref) = refs[n_slabs:]
    tm = cos_ref.shape[0]
    n = tm // d
    hw = HEADS * HEAD_DIM

    for q in range(tm // ROPE_BLOCK):
        blk = slice(q * ROPE_BLOCK, (q + 1) * ROPE_BLOCK)
        ca = ca_ref[q:q + 1, :]
        sa = sa_ref[q:q + 1, :]
        cos_ref[blk, :] = ca * cb_ref[...] - sa * sb_ref[...]
        sin_ref[blk, :] = sa * cbs_ref[...] + ca * sbs_ref[...]

    def residues(src_ref):
        if d % 8:
            return [src_ref[_residue_rows(r, n, d), :] for r in range(d)]
        for a in range(4):
            tmp_ref[a] = src_ref[pl.ds(a, tm // 4, stride=4), :]
        return [tmp_ref[r % 4, pl.ds(r // 4, n, stride=d // 4), :] for r in range(d)]

    for c, x_ref in enumerate(x_refs):
        for r, piece in enumerate(residues(x_ref)):
            xb_ref[r * n:(r + 1) * n, c * LANES:(c + 1) * LANES] = piece.astype(BF16)
    for src_ref, dst_ref in ((cos_ref, cs_ref), (sin_ref, sn_ref)):
        for r, piece in enumerate(residues(src_ref)):
            dst_ref[r * n:(r + 1) * n, :] = piece
    xb = xb_ref[...]

    for kind in range(3):
        acc = _dot(xb, w_ref[:, kind * hw:(kind + 1) * hw])
        if kind == 2:
            for r in range(d):
                o_ref[r, :, kind * hw:(kind + 1) * hw] = acc[r * n:(r + 1) * n, :].astype(BF16)
            continue
        scale = HEAD_DIM ** -0.5 * LOG2E if kind == 0 else 1.0
        cos = cs_ref[...] * scale
        sin = sn_ref[...] * scale
        for h in range(HEADS):
            t = acc[:, h * HEAD_DIM:(h + 1) * HEAD_DIM]
            rot = (t * cos + pltpu.roll(t, HEAD_DIM // 2, axis=1) * sin).astype(BF16)
            lanes = slice(kind * hw + h * HEAD_DIM, kind * hw + (h + 1) * HEAD_DIM)
            for r in range(d):
                o_ref[r, :, lanes] = rot[r * n:(r + 1) * n, :]


def _qkv_rope(x2d, w_in_bf16, rope, g, d, batch, seq, tm):
    d_model = x2d.shape[1]
    hw = HEADS * HEAD_DIM
    seq_tiles = seq // tm
    n = tm // d
    coarse = pl.BlockSpec((tm // ROPE_BLOCK, HEAD_DIM), lambda i: (i % seq_tiles, 0))
    fine = pl.BlockSpec((ROPE_BLOCK, HEAD_DIM), lambda i: (0, 0))
    return pl.pallas_call(
        functools.partial(_qkv_kernel, d=d),
        out_shape=jax.ShapeDtypeStruct((batch, d, seq // d, 3 * hw), BF16),
        grid=(batch * seq_tiles,),
        in_specs=[pl.BlockSpec((tm, LANES), lambda i, c=c: (i, c))
                  for c in range(d_model // LANES)] + [
            pl.BlockSpec((d_model, 3 * hw), lambda i: (0, g)),
            coarse, coarse, fine, fine, fine, fine,
        ],
        out_specs=pl.BlockSpec((None, d, n, 3 * hw),
                               lambda i: (i // seq_tiles, 0, i % seq_tiles, 0)),
        scratch_shapes=[pltpu.VMEM((tm, d_model), BF16),
                        pltpu.VMEM((tm, HEAD_DIM), F32),
                        pltpu.VMEM((tm, HEAD_DIM), F32),
                        pltpu.VMEM((tm, HEAD_DIM), F32),
                        pltpu.VMEM((tm, HEAD_DIM), F32),
                        pltpu.VMEM((4, tm // 4, LANES), F32)],
        compiler_params=_params("parallel"),
        name=f"qkv_rope_g{g}",
    )(*([x2d] * (d_model // LANES)), w_in_bf16, *rope)


def _attn_kernel(q_ref, k_ref, v_ref, kp_ref, vp_ref, o_ref, lse_ref, *, nblk):
    m = pl.program_id(2)
    blk_rows = ATTN_BLOCK
    row = lax.broadcasted_iota(jnp.int32, (blk_rows, 2 * blk_rows), 0)
    col = lax.broadcasted_iota(jnp.int32, (blk_rows, 2 * blk_rows), 1)
    band = jnp.logical_and(col >= row, col <= row + blk_rows)
    first_band = jnp.logical_and(band, jnp.logical_or(col >= blk_rows, m > 0))
    lane = lax.broadcasted_iota(jnp.int32, (blk_rows, LANES), 1)
    ones = jnp.ones((2 * blk_rows, HEAD_DIM), BF16)

    def block(ri, q_rows, keys, vals, mask):
        mx_tile = jnp.zeros((blk_rows, LANES), F32)
        l_tile = jnp.ones((blk_rows, LANES), F32)
        for h in range(HEADS):
            cols = slice(h * HEAD_DIM, (h + 1) * HEAD_DIM)
            s = jnp.where(mask, _dot_nt(q_ref[ri, q_rows, cols], keys(cols)), NEG_INF)
            mx = jnp.max(jnp.maximum(s[:, :blk_rows], s[:, blk_rows:]), axis=1, keepdims=True)
            p = jnp.exp2(s - mx).astype(BF16)
            oe = _dot(p, jnp.concatenate([vals(cols), ones], axis=1))
            l = oe[:, HEAD_DIM:]
            o_ref[ri, q_rows, cols] = (oe[:, :HEAD_DIM] / l).astype(BF16)
            mx_tile = jnp.where(lane == h, mx, mx_tile)
            l_tile = jnp.where(lane == h, l, l_tile)
        lse_ref[ri, q_rows, :] = mx_tile * LN2 + jnp.log(l_tile)

    head = slice(0, blk_rows)
    for ri in range(q_ref.shape[0]):
        block(ri, head,
              lambda cols: jnp.concatenate([kp_ref[ri, :, cols], k_ref[ri, head, cols]], axis=0),
              lambda cols: jnp.concatenate([vp_ref[ri, :, cols], v_ref[ri, head, cols]], axis=0),
              first_band)
        for blk in range(1, nblk):
            q_rows = slice(blk * blk_rows, (blk + 1) * blk_rows)
            kv_rows = slice((blk - 1) * blk_rows, (blk + 1) * blk_rows)
            block(ri, q_rows, lambda cols: k_ref[ri, kv_rows, cols],
                  lambda cols: v_ref[ri, kv_rows, cols], band)


def _dilated_attention_group(qkv_g, g, rows):
    batch, d, sub_len, _ = qkv_g.shape
    hw = HEADS * HEAD_DIM
    res = max(1, min(d, rows // sub_len))
    rows = min(rows, sub_len)
    nblk = rows // ATTN_BLOCK

    def main_spec(kind):
        return pl.BlockSpec((None, res, rows, hw), lambda b, r, m: (b, r, m, kind))

    def prev_spec(kind):
        return pl.BlockSpec((None, res, ATTN_BLOCK, hw),
                            lambda b, r, m: (b, r, jnp.maximum(m * nblk - 1, 0), kind))

    return pl.pallas_call(
        functools.partial(_attn_kernel, nblk=nblk),
        out_shape=(jax.ShapeDtypeStruct((batch, d, sub_len, hw), BF16),
                   jax.ShapeDtypeStruct((batch, d, sub_len, LANES), F32)),
        grid=(batch, d // res, sub_len // rows),
        in_specs=[main_spec(0), main_spec(1), main_spec(2), prev_spec(1), prev_spec(2)],
        out_specs=(pl.BlockSpec((None, res, rows, hw), lambda b, r, m: (b, r, m, 0)),
                   pl.BlockSpec((None, res, rows, LANES), lambda b, r, m: (b, r, m, 0))),
        compiler_params=_params("parallel", "parallel", "arbitrary"),
        name=f"dilated_attn_g{g}",
    )(qkv_g, qkv_g, qkv_g, qkv_g, qkv_g)


def _merge_out_kernel(o0_ref, o1_ref, o2_ref, l0_ref, l1_ref, l2_ref, e_ref, x_ref, w_ref,
                      g_ref, b_ref, out_ref, on_ref, ln_ref):
    tm = x_ref.shape[0]
    outs, lses = [], []
    for gi, (o_ref, l_ref) in enumerate(((o0_ref, l0_ref), (o1_ref, l1_ref), (o2_ref, l2_ref))):
        d = o_ref.shape[0]
        n = tm // d
        if d == 1:
            outs.append(o_ref[0].astype(F32))
            lses.append(l_ref[0])
            continue
        for r in range(d):
            rows = _residue_rows(r, n, d)
            o_r = o_ref[r].astype(F32)
            for c in range(on_ref.shape[1]):
                on_ref[gi, c, rows, :] = o_r[:, c * LANES:(c + 1) * LANES]
            ln_ref[gi, rows, :] = l_ref[r]
        outs.append(jnp.concatenate([on_ref[gi, c] for c in range(on_ref.shape[1])], axis=1))
        lses.append(ln_ref[gi])
    mx = jnp.maximum(jnp.maximum(lses[0], lses[1]), lses[2])
    ws = [jnp.exp(l - mx) for l in lses]
    inv = 1.0 / (ws[0] + ws[1] + ws[2])
    e = e_ref[...]
    merged = None
    for w, o in zip(ws, outs):
        w = w * inv
        w_hi = w.astype(BF16)
        w_lo = (w - w_hi.astype(F32)).astype(BF16)
        w_full = _dot(jnp.concatenate([w_hi, w_lo], axis=1), e)
        term = w_full * o
        merged = term if merged is None else merged + term
    y = _dot(merged.astype(BF16), w_ref[...])
    r = DEEPNORM_ALPHA * x_ref[...] + y
    out_ref[...] = _layer_norm(r, g_ref[...], b_ref[...])


def _merge_out(os, lses, x2d, w_out_bf16, ln_g, ln_b, seq, tm):
    tokens, d_model = x2d.shape
    hw = HEADS * HEAD_DIM
    seq_tiles = seq // tm
    expand = (jnp.arange(2 * LANES)[:, None] % LANES == (jnp.arange(hw)[None, :] // HEAD_DIM)).astype(BF16)
    row = lambda i: (i, 0)
    fixed = lambda i: (0, 0)

    def group_spec(arr):
        d, width = arr.shape[1], arr.shape[3]
        return pl.BlockSpec((None, d, tm // d, width),
                            lambda i: (i // seq_tiles, 0, i % seq_tiles, 0))

    return pl.pallas_call(
        _merge_out_kernel,
        out_shape=jax.ShapeDtypeStruct((tokens, d_model), F32),
        grid=(tokens // tm,),
        in_specs=[group_spec(a) for a in os] + [group_spec(a) for a in lses] + [
            pl.BlockSpec((2 * LANES, hw), fixed),
            pl.BlockSpec((tm, d_model), row),
            pl.BlockSpec((hw, d_model), fixed),
            pl.BlockSpec((1, d_model), fixed),
            pl.BlockSpec((1, d_model), fixed),
        ],
        out_specs=pl.BlockSpec((tm, d_model), row),
        scratch_shapes=[pltpu.VMEM((N_GROUPS, hw // LANES, tm, LANES), F32),
                        pltpu.VMEM((N_GROUPS, tm, LANES), F32)],
        compiler_params=_params("parallel"),
        name="attn_merge_out_ln",
    )(*os, *lses, expand, x2d, w_out_bf16, ln_g.reshape(1, -1), ln_b.reshape(1, -1))


def _ffn_kernel(x_ref, halo_ref, wup_ref, cw_ref, cb_ref, wdown_ref, g_ref, b_ref, out_ref,
                xe_ref, h_ref, *, seq_tiles, d_ff, chunk, ln_split):
    i = pl.program_id(0)
    x = x_ref[...]
    first = (i % seq_tiles) == 0
    xe_ref[0:SUBLANES, :] = jnp.where(first, 0.0, halo_ref[...]).astype(BF16)
    xe_ref[SUBLANES:, :] = x.astype(BF16)
    xe = xe_ref[...]
    for c in range(d_ff // chunk):
        gcols = slice(c * chunk, (c + 1) * chunk)
        vcols = slice(d_ff + c * chunk, d_ff + (c + 1) * chunk)
        gate = _dot(xe, wup_ref[:, gcols])
        val = _dot(xe[SUBLANES:, :], wup_ref[:, vcols])
        conv = (cb_ref[:, gcols]
                + cw_ref[0:1, gcols] * pltpu.roll(gate, 2, axis=0)[SUBLANES:, :]
                + cw_ref[1:2, gcols] * pltpu.roll(gate, 1, axis=0)[SUBLANES:, :]
                + cw_ref[2:3, gcols] * gate[SUBLANES:, :])
        act = conv * (1.0 / (1.0 + jnp.exp(-conv)))
        h_ref[:, gcols] = (act * val).astype(BF16)
    part = x_ref.shape[0] // ln_split
    for s in range(ln_split):
        rows = slice(s * part, (s + 1) * part)
        f = _dot(h_ref[rows, :], wdown_ref[...])
        r = DEEPNORM_ALPHA * x_ref[rows, :] + f
        out_ref[rows, :] = _layer_norm(r, g_ref[...], b_ref[...])


def _conv_ffn(x2d, w_up_bf16, conv_w, conv_b, w_down_bf16, ln_g, ln_b, layer, seq, tm, chunk):
    tokens, d_model = x2d.shape
    d_ff = w_down_bf16.shape[1]
    seq_tiles = seq // tm
    halo_blocks = tm // SUBLANES
    fixed = lambda i: (0, 0)
    of_layer = lambda i: (layer, 0, 0)
    single = dict(pipeline_mode=pl.Buffered(1))
    return pl.pallas_call(
        functools.partial(_ffn_kernel, seq_tiles=seq_tiles, d_ff=d_ff, chunk=chunk, ln_split=4),
        out_shape=jax.ShapeDtypeStruct((tokens, d_model), F32),
        grid=(tokens // tm,),
        in_specs=[
            pl.BlockSpec((tm, d_model), lambda i: (i, 0)),
            pl.BlockSpec((SUBLANES, d_model), lambda i: (jnp.maximum(i * halo_blocks - 1, 0), 0)),
            pl.BlockSpec((None, d_model, 2 * d_ff), of_layer, **single),
            pl.BlockSpec((CONV_WIDTH, d_ff), fixed),
            pl.BlockSpec((1, d_ff), fixed),
            pl.BlockSpec((None, d_ff, d_model), of_layer, **single),
            pl.BlockSpec((1, d_model), fixed),
            pl.BlockSpec((1, d_model), fixed),
        ],
        out_specs=pl.BlockSpec((tm, d_model), lambda i: (i, 0)),
        scratch_shapes=[pltpu.VMEM((tm + SUBLANES, d_model), BF16),
                        pltpu.VMEM((tm, d_ff), BF16)],
        compiler_params=_params("parallel"),
        name="conv_ffn_ln",
    )(x2d, x2d, w_up_bf16, conv_w, conv_b.reshape(1, -1), w_down_bf16,
      ln_g.reshape(1, -1), ln_b.reshape(1, -1))


def _ssm_in_kernel(x_ref, wt_ref, ut_ref):
    gc = SSM_GROUP_CH
    u = _dot_nt(wt_ref[...], x_ref[...].astype(BF16))
    for g in range(ut_ref.shape[0]):
        for q in range(ut_ref.shape[1] // gc):
            ut_ref[g, q * gc:(q + 1) * gc, :] = u[g * gc:(g + 1) * gc, q * SSM_CHUNK:(q + 1) * SSM_CHUNK]


def _ssm_in(x2d, w_in_t_bf16, tn):
    tokens, d_model = x2d.shape
    d_ssm = w_in_t_bf16.shape[0]
    gc = SSM_GROUP_CH
    groups = d_ssm // gc
    return pl.pallas_call(
        _ssm_in_kernel,
        out_shape=jax.ShapeDtypeStruct((groups, tokens // SSM_CHUNK * gc, SSM_CHUNK), F32),
        grid=(tokens // tn,),
        in_specs=[pl.BlockSpec((tn, d_model), lambda i: (i, 0)),
                  pl.BlockSpec((d_ssm, d_model), lambda i: (0, 0))],
        out_specs=pl.BlockSpec((groups, tn // SSM_CHUNK * gc, SSM_CHUNK), lambda i: (0, i, 0)),
        compiler_params=_params("parallel"),
        name="ssm_in_proj",
    )(x2d, w_in_t_bf16)


def _ssm_core_kernel(ut_ref, kv0_ref, kvn_ref, rpt_ref, fpt_ref, ba_ref, bb_ref, ca_ref, cb_ref,
                     la_ref, lb_ref, yt_ref, toep0_ref, toep1_ref, lhs_ref, pm_ref, cmt_ref, yacc_ref,
                     tmp_ref, *, chunks_per_seq):
    gc = SSM_GROUP_CH
    L = SSM_CHUNK
    rows = ut_ref.shape[0] // gc
    n2 = 2 * SSM_STATE
    pair = 2 * L
    steps = gc // 2
    g = pl.program_id(0)
    slot = g % 2

    srow = lax.broadcasted_iota(jnp.int32, (L, L), 0)
    tcol = lax.broadcasted_iota(jnp.int32, (L, L), 1)
    causal = tcol >= srow

    def build_pair(kv_ref, i, dst_ref):
        for cc in range(2):
            c = 2 * i + cc
            for cp in range(gc):
                kv = kv_ref[c, pl.ds(cp, 1), :]
                blk = pltpu.roll(jnp.broadcast_to(kv, (L, L)), 0, 1, stride=1, stride_axis=0)
                blk = jnp.where(causal, blk, 0.0)
                dst_ref[pl.ds(pl.multiple_of(c * L, L), L), cp * L:(cp + 1) * L] = blk.astype(BF16)

    @pl.when(g == 0)
    def _():
        def first(i, carry):
            build_pair(kv0_ref, i, toep0_ref)
            return carry
        lax.fori_loop(0, steps, first, 0)

    quarter = rows * gc // 4
    for a in range(4):
        tmp_ref[a] = ut_ref[pl.ds(a, quarter, stride=4), :]
    for c in range(gc):
        piece = tmp_ref[c % 4, pl.ds(c // 4, rows, stride=4), :].astype(BF16)
        lhs_ref[c // 2, :, (c % 2) * L:(c % 2 + 1) * L] = piece

    rpt = rpt_ref[...]
    rpt_sw = pltpu.roll(rpt, SSM_STATE, axis=1)
    fpt = fpt_ref[...]
    fpt_sw = pltpu.roll(fpt, SSM_STATE, axis=1)
    for c in range(gc):
        pm_ref[c * L:(c + 1) * L, :] = (rpt * ba_ref[c:c + 1, :] + rpt_sw * bb_ref[c:c + 1, :]).astype(BF16)
        cmt_ref[c * L:(c + 1) * L, :] = (fpt * ca_ref[c:c + 1, :] + fpt_sw * cb_ref[c:c + 1, :]).astype(BF16)

    st = None
    for i in range(steps):
        part = _dot(lhs_ref[i], pm_ref[i * pair:(i + 1) * pair, :])
        st = part if st is None else st + part

    a = la_ref[...]
    b = lb_ref[...]
    jrow = lax.broadcasted_iota(jnp.int32, (rows, n2), 0) % chunks_per_seq
    sh = 1
    while sh < chunks_per_seq:
        prev = jnp.where(jrow >= sh, pltpu.roll(st, sh, axis=0), 0.0)
        st = st + prev * a + pltpu.roll(prev, SSM_STATE, axis=1) * b
        a, b = a * a - b * b, 2.0 * a * b
        sh *= 2
    carried = jnp.where(jrow >= 1, pltpu.roll(st, 1, axis=0), 0.0)
    yacc_ref[...] = _dot_nt(carried.astype(BF16), cmt_ref[...])

    def pipeline(cur_ref, nxt_ref):
        def step(i, carry):
            build_pair(kvn_ref, i, nxt_ref)
            rws = pl.ds(pl.multiple_of(i * pair, pair), pair)
            yacc_ref[...] += _dot(lhs_ref[i], cur_ref[rws, :])
            return carry
        lax.fori_loop(0, steps, step, 0)

    @pl.when(slot == 0)
    def _():
        pipeline(toep0_ref, toep1_ref)

    @pl.when(slot == 1)
    def _():
        pipeline(toep1_ref, toep0_ref)

    for cp in range(gc):
        tmp_ref[cp % 4, pl.ds(cp // 4, rows, stride=4), :] = yacc_ref[:, cp * L:(cp + 1) * L]
    for a4 in range(4):
        yt_ref[pl.ds(a4, quarter, stride=4), :] = tmp_ref[a4]


def _ssm_core(ut, kvec, rpt, fpt, ba, bb, ca, cb, lam_a, lam_b, chunks_per_seq):
    groups, group_rows, L = ut.shape
    gc, n2 = SSM_GROUP_CH, 2 * SSM_STATE
    rows = group_rows // gc
    per_group = lambda g: (g, 0, 0)
    return pl.pallas_call(
        functools.partial(_ssm_core_kernel, chunks_per_seq=chunks_per_seq),
        out_shape=jax.ShapeDtypeStruct((groups, group_rows, L), F32),
        grid=(groups,),
        in_specs=[
            pl.BlockSpec((None, group_rows, L), per_group),
            pl.BlockSpec((None, gc, gc, L), lambda g: (0, 0, 0, 0)),
            pl.BlockSpec((None, gc, gc, L), lambda g: (jnp.minimum(g + 1, groups - 1), 0, 0, 0)),
            pl.BlockSpec((None, L, n2), per_group),
            pl.BlockSpec((None, L, n2), per_group),
            pl.BlockSpec((None, gc, n2), per_group),
            pl.BlockSpec((None, gc, n2), per_group),
            pl.BlockSpec((None, gc, n2), per_group),
            pl.BlockSpec((None, gc, n2), per_group),
            pl.BlockSpec((None, 1, n2), per_group),
            pl.BlockSpec((None, 1, n2), per_group),
        ],
        out_specs=pl.BlockSpec((None, group_rows, L), per_group),
        scratch_shapes=[pltpu.VMEM((gc * L, gc * L), BF16),
                        pltpu.VMEM((gc * L, gc * L), BF16),
                        pltpu.VMEM((gc // 2, rows, 2 * L), BF16),
                        pltpu.VMEM((gc * L, n2), BF16),
                        pltpu.VMEM((gc * L, n2), BF16),
                        pltpu.VMEM((rows, gc * L), F32),
                        pltpu.VMEM((4, rows * gc // 4, L), F32)],
        compiler_params=_params("arbitrary"),
        name="ssm_chunk_conv",
    )(ut, kvec, kvec, rpt, fpt, ba, bb, ca, cb, lam_a, lam_b)


def _ssm_operators(a_re, a_im, log_dt, b_re, b_im, c_re, c_im):
    L = SSM_CHUNK
    hi = lax.Precision.HIGHEST
    a_re = a_re.astype(F32)
    a_im = a_im.astype(F32)
    dt = jnp.exp(log_dt.astype(F32))[:, None]
    mag = jnp.exp(a_re * dt)
    lam_re = mag * jnp.cos(a_im * dt)
    lam_im = mag * jnp.sin(a_im * dt)
    nr, ni = lam_re - 1.0, lam_im
    den = a_re * a_re + a_im * a_im
    coef_re = ((nr * a_re + ni * a_im) / den)[..., None]
    coef_im = ((ni * a_re - nr * a_im) / den)[..., None]
    b_re = b_re.astype(F32)
    b_im = b_im.astype(F32)
    bb_re = coef_re * b_re - coef_im * b_im
    bb_im = coef_re * b_im + coef_im * b_re
    cr = jnp.swapaxes(c_re.astype(F32), 1, 2)
    ci = jnp.swapaxes(c_im.astype(F32), 1, 2)

    k = jnp.arange(L + 1, dtype=F32)
    pmag = jnp.exp((a_re * dt)[..., None] * k)
    ang = (a_im * dt)[..., None] * k
    pw_re = pmag * jnp.cos(ang)
    pw_im = pmag * jnp.sin(ang)

    cb_re = cr[:, :, None, :] * bb_re[..., None] - ci[:, :, None, :] * bb_im[..., None]
    cb_im = cr[:, :, None, :] * bb_im[..., None] + ci[:, :, None, :] * bb_re[..., None]
    kvec = (jnp.einsum('gncd,gnk->gcdk', cb_re, pw_re[..., :L], precision=hi)
            - jnp.einsum('gncd,gnk->gcdk', cb_im, pw_im[..., :L], precision=hi))

    def lanes(re, im):
        return jnp.concatenate([jnp.swapaxes(re, 1, 2), jnp.swapaxes(im, 1, 2)], axis=-1)

    rpt = lanes(jnp.flip(pw_re[..., :L], axis=-1), jnp.flip(pw_im[..., :L], axis=-1))
    fpt = lanes(pw_re[..., 1:], pw_im[..., 1:])
    ba = lanes(bb_re, bb_re)
    bb = lanes(-bb_im, bb_im)
    ca = lanes(cr, -cr)
    cb = lanes(-ci, -ci)
    lam_a = jnp.concatenate([pw_re[..., L], pw_re[..., L]], axis=-1)[:, None, :]
    lam_b = jnp.concatenate([-pw_im[..., L], pw_im[..., L]], axis=-1)[:, None, :]
    return kvec, rpt, fpt, ba, bb, ca, cb, lam_a, lam_b


def _ssm_out_kernel(yt_ref, ut_ref, d_ref, wg_ref, bg_ref, wo_ref, x_ref, g_ref, b_ref, out_ref, *, sub):
    gc = SSM_GROUP_CH
    groups, nq = yt_ref.shape[0], yt_ref.shape[1] // gc
    qs = sub // SSM_CHUNK
    n_sub = nq // qs

    def channel_major(ref, q0):
        return jnp.concatenate(
            [jnp.concatenate([ref[g, q * gc:(q + 1) * gc, :] for g in range(groups)], axis=0)
             for q in range(q0, q0 + qs)], axis=1)

    def gelu_gate(s):
        y = channel_major(yt_ref, s * qs) + d_ref[...] * channel_major(ut_ref, s * qs)
        z = 0.5 * y * (1.0 + jnp.tanh(math.sqrt(2.0 / math.pi) * (y + 0.044715 * (y * y * y))))
        return z, _dot(wg_ref[...], z.astype(BF16)) + bg_ref[...]

    def glu_out(s, z, gate):
        rows = slice(s * sub, (s + 1) * sub)
        zs = z * (1.0 / (1.0 + jnp.exp(-gate)))
        f = _dot_tn(zs.astype(BF16), wo_ref[...])
        r = DEEPNORM_ALPHA * x_ref[rows, :] + f
        out_ref[rows, :] = _layer_norm(r, g_ref[...], b_ref[...])

    pending = gelu_gate(0)
    for s in range(n_sub):
        upcoming = gelu_gate(s + 1) if s + 1 < n_sub else None
        glu_out(s, *pending)
        pending = upcoming


def _ssm_out(yt, ut, d_skip, w_glu_t_bf16, b_glu, w_out_bf16, x2d, ln_g, ln_b, tn, sub):
    tokens, d_model = x2d.shape
    groups = yt.shape[0]
    d_ssm = groups * SSM_GROUP_CH
    fixed = lambda i: (0, 0)
    chunked = pl.BlockSpec((groups, tn // SSM_CHUNK * SSM_GROUP_CH, SSM_CHUNK), lambda i: (0, i, 0))
    return pl.pallas_call(
        functools.partial(_ssm_out_kernel, sub=sub),
        out_shape=jax.ShapeDtypeStruct((tokens, d_model), F32),
        grid=(tokens // tn,),
        in_specs=[
            chunked,
            chunked,
            pl.BlockSpec((d_ssm, 1), fixed),
            pl.BlockSpec((d_ssm, d_ssm), fixed),
            pl.BlockSpec((d_ssm, 1), fixed),
            pl.BlockSpec((d_ssm, d_model), fixed),
            pl.BlockSpec((tn, d_model), lambda i: (i, 0)),
            pl.BlockSpec((1, d_model), fixed),
            pl.BlockSpec((1, d_model), fixed),
        ],
        out_specs=pl.BlockSpec((tn, d_model), lambda i: (i, 0)),
        compiler_params=_params("parallel"),
        name="ssm_glu_out_ln",
    )(yt, ut, d_skip.reshape(-1, 1), w_glu_t_bf16, b_glu.reshape(-1, 1), w_out_bf16, x2d,
      ln_g.reshape(1, -1), ln_b.reshape(1, -1))


def _rope_tables(seq):
    inv_freq = ROPE_THETA ** (-jnp.arange(0, HEAD_DIM, 2, dtype=F32) / HEAD_DIM)
    inv_freq = jnp.concatenate([inv_freq, inv_freq])[None, :]
    ang_a = (jnp.arange(seq // ROPE_BLOCK, dtype=F32) * ROPE_BLOCK)[:, None] * inv_freq
    ang_b = jnp.arange(ROPE_BLOCK, dtype=F32)[:, None] * inv_freq
    sign = jnp.where(jnp.arange(HEAD_DIM) < HEAD_DIM // 2, -1.0, 1.0).astype(F32)[None, :]
    cb, sb = jnp.cos(ang_b), jnp.sin(ang_b)
    return jnp.cos(ang_a), jnp.sin(ang_a), cb, sb, cb * sign, sb * sign


def _row_tile(seq, want):
    tm = min(want, seq)
    assert seq % tm == 0
    return tm


def kernel(x, attn_w_in, attn_w_out, ssm_w_in, ssm_a_re, ssm_a_im, ssm_log_dt, ssm_b_re, ssm_b_im,
           ssm_c_re, ssm_c_im, ssm_d, ssm_w_glu, ssm_b_glu, ssm_w_out, ffn_w_up, ffn_conv_w,
           ffn_conv_b, ffn_w_down, ln_g, ln_b):
    batch, seq, d_model = x.shape
    assert seq % DILATION_PAIRS[-1][0] == 0 and seq % SSM_CHUNK == 0
    assert all(w // d == ATTN_BLOCK for w, d in DILATION_PAIRS)
    h = x.reshape(batch * seq, d_model)
    ffn_chunk = 256

    w_up_all = ffn_w_up.astype(BF16)
    w_down_all = ffn_w_down.astype(BF16)

    def ffn(h, i):
        return _conv_ffn(h, w_up_all, ffn_conv_w[i], ffn_conv_b[i], w_down_all, ln_g[i, 1], ln_b[i, 1],
                         i, seq, _row_tile(seq, 1024), ffn_chunk)

    rope = _rope_tables(seq)
    w_in = attn_w_in[0].astype(BF16)
    os, lses = [], []
    for g, (_, dilation) in enumerate(DILATION_PAIRS):
        qkv_g = _qkv_rope(h, w_in, rope, g, dilation, batch, seq, _row_tile(seq, 1024))
        o, lse = _dilated_attention_group(qkv_g, g, 1024)
        os.append(o)
        lses.append(lse)
    h = _merge_out(os, lses, h, attn_w_out[0].astype(BF16), ln_g[0, 0], ln_b[0, 0], seq,
                   _row_tile(seq, 512))
    h = ffn(h, 0)

    ssm_ops = _ssm_operators(
        ssm_a_re[0], ssm_a_im[0], ssm_log_dt[0], ssm_b_re[0], ssm_b_im[0], ssm_c_re[0], ssm_c_im[0])
    ut = _ssm_in(h, ssm_w_in[0].T.astype(BF16), _row_tile(seq, 1024))
    yt = _ssm_core(ut, *ssm_ops, seq // SSM_CHUNK)
    h = _ssm_out(yt, ut, ssm_d[0], ssm_w_glu[0].T.astype(BF16), ssm_b_glu[0],
                 ssm_w_out[0].astype(BF16), h, ln_g[1, 0], ln_b[1, 0], _row_tile(seq, 1024), 512)
    h = ffn(h, 1)
    return h.reshape(batch, seq, d_model)
```

```python
import functools
import math

import jax
import jax.numpy as jnp
from jax import lax
from jax.experimental import pallas as pl
from jax.experimental.pallas import tpu as pltpu

F32 = jnp.float32
BF16 = jnp.bfloat16

DEPTH = 2
DILATION_PAIRS = ((128, 1), (512, 4), (2048, 16))
N_GROUPS = len(DILATION_PAIRS)
HEADS = 8
HEAD_DIM = 128
ROPE_THETA = 10000.0
SSM_GROUP_CH = 16
SSM_STATE = 64
CONV_WIDTH = 3
DEEPNORM_ALPHA = (2.0 * DEPTH) ** 0.25
LN_EPS = 1e-5
NEG_INF = -1e30
LOG2E = math.log2(math.e)
LN2 = math.log(2.0)

LANES = 128
SUBLANES = 8
VMEM_LIMIT_BYTES = 56 * 1024 * 1024

ATTN_BLOCK = 128
ROPE_BLOCK = 128
SSM_CHUNK = 128


def _params(*semantics):
    return pltpu.CompilerParams(dimension_semantics=semantics,
                                vmem_limit_bytes=VMEM_LIMIT_BYTES)


def _layer_norm(r, g, b):
    mu = jnp.mean(r, axis=-1, keepdims=True)
    d = r - mu
    var = jnp.mean(d * d, axis=-1, keepdims=True)
    return d * lax.rsqrt(var + LN_EPS) * g + b


def _dot(a, b):
    return jnp.dot(a, b, preferred_element_type=F32)


def _dot_nt(a, b):
    return lax.dot_general(a, b, (((1,), (1,)), ((), ())), preferred_element_type=F32)


def _dot_tn(a, b):
    return lax.dot_general(a, b, (((0,), (0,)), ((), ())), preferred_element_type=F32)


def _residue_rows(r, n, d):
    return pl.ds(r, n, stride=d) if d > 1 else slice(None)


def _qkv_kernel(*refs, d):
    n_slabs = len(refs) - 14
    x_refs = refs[:n_slabs]
    (w_ref, ca_ref, sa_ref, cb_ref, sb_ref, cbs_ref, sbs_ref, o_ref,
     xb_ref, cos_ref, sin_ref, cs_ref, sn_ref, tmp_ref) = refs[n_slabs:]
    tm = cos_ref.shape[0]
    n = tm // d
    hw = HEADS * HEAD_DIM

    for q in range(tm // ROPE_BLOCK):
        blk = slice(q * ROPE_BLOCK, (q + 1) * ROPE_BLOCK)
        ca = ca_ref[q:q + 1, :]
        sa = sa_ref[q:q + 1, :]
        cos_ref[blk, :] = ca * cb_ref[...] - sa * sb_ref[...]
        sin_ref[blk, :] = sa * cbs_ref[...] + ca * sbs_ref[...]

    def residues(src_ref):
        if d % 8:
            return [src_ref[_residue_rows(r, n, d), :] for r in range(d)]
        for a in range(4):
            tmp_ref[a] = src_ref[pl.ds(a, tm // 4, stride=4), :]
        return [tmp_ref[r % 4, pl.ds(r // 4, n, stride=d // 4), :] for r in range(d)]

    for c, x_ref in enumerate(x_refs):
        for r, piece in enumerate(residues(x_ref)):
            xb_ref[r * n:(r + 1) * n, c * LANES:(c + 1) * LANES] = piece.astype(BF16)
    for src_ref, dst_ref in ((cos_ref, cs_ref), (sin_ref, sn_ref)):
        for r, piece in enumerate(residues(src_ref)):
            dst_ref[r * n:(r + 1) * n, :] = piece
    xb = xb_ref[...]

    for kind in range(3):
        acc = _dot(xb, w_ref[:, kind * hw:(kind + 1) * hw])
        if kind == 2:
            for r in range(d):
                o_ref[r, :, kind * hw:(kind + 1) * hw] = acc[r * n:(r + 1) * n, :].astype(BF16)
            continue
        scale = HEAD_DIM ** -0.5 * LOG2E if kind == 0 else 1.0
        cos = cs_ref[...] * scale
        sin = sn_ref[...] * scale
        for h in range(HEADS):
            t = acc[:, h * HEAD_DIM:(h + 1) * HEAD_DIM]
            rot = (t * cos + pltpu.roll(t, HEAD_DIM // 2, axis=1) * sin).astype(BF16)
            lanes = slice(kind * hw + h * HEAD_DIM, kind * hw + (h + 1) * HEAD_DIM)
            for r in range(d):
                o_ref[r, :, lanes] = rot[r * n:(r + 1) * n, :]


def _qkv_rope(x2d, w_in_bf16, rope, g, d, batch, seq, tm):
    d_model = x2d.shape[1]
    hw = HEADS * HEAD_DIM
    seq_tiles = seq // tm
    n = tm // d
    coarse = pl.BlockSpec((tm // ROPE_BLOCK, HEAD_DIM), lambda i: (i % seq_tiles, 0))
    fine = pl.BlockSpec((ROPE_BLOCK, HEAD_DIM), lambda i: (0, 0))
    return pl.pallas_call(
        functools.partial(_qkv_kernel, d=d),
        out_shape=jax.ShapeDtypeStruct((batch, d, seq // d, 3 * hw), BF16),
        grid=(batch * seq_tiles,),
        in_specs=[pl.BlockSpec((tm, LANES), lambda i, c=c: (i, c))
                  for c in range(d_model // LANES)] + [
            pl.BlockSpec((d_model, 3 * hw), lambda i: (0, g)),
            coarse, coarse, fine, fine, fine, fine,
        ],
        out_specs=pl.BlockSpec((None, d, n, 3 * hw),
                               lambda i: (i // seq_tiles, 0, i % seq_tiles, 0)),
        scratch_shapes=[pltpu.VMEM((tm, d_model), BF16),
                        pltpu.VMEM((tm, HEAD_DIM), F32),
                        pltpu.VMEM((tm, HEAD_DIM), F32),
                        pltpu.VMEM((tm, HEAD_DIM), F32),
                        pltpu.VMEM((tm, HEAD_DIM), F32),
                        pltpu.VMEM((4, tm // 4, LANES), F32)],
        compiler_params=_params("parallel"),
        name=f"qkv_rope_g{g}",
    )(*([x2d] * (d_model // LANES)), w_in_bf16, *rope)


def _attn_kernel(q_ref, k_ref, v_ref, kp_ref, vp_ref, o_ref, lse_ref, *, nblk):
    m = pl.program_id(2)
    blk_rows = ATTN_BLOCK
    row = lax.broadcasted_iota(jnp.int32, (blk_rows, 2 * blk_rows), 0)
    col = lax.broadcasted_iota(jnp.int32, (blk_rows, 2 * blk_rows), 1)
    band = jnp.logical_and(col >= row, col <= row + blk_rows)
    first_band = jnp.logical_and(band, jnp.logical_or(col >= blk_rows, m > 0))
    lane = lax.broadcasted_iota(jnp.int32, (blk_rows, LANES), 1)
    ones = jnp.ones((2 * blk_rows, HEAD_DIM), BF16)

    def block(ri, q_rows, keys, vals, mask):
        mx_tile = jnp.zeros((blk_rows, LANES), F32)
        l_tile = jnp.ones((blk_rows, LANES), F32)
        for h in range(HEADS):
            cols = slice(h * HEAD_DIM, (h + 1) * HEAD_DIM)
            s = jnp.where(mask, _dot_nt(q_ref[ri, q_rows, cols], keys(cols)), NEG_INF)
            mx = jnp.max(jnp.maximum(s[:, :blk_rows], s[:, blk_rows:]), axis=1, keepdims=True)
            p = jnp.exp2(s - mx).astype(BF16)
            oe = _dot(p, jnp.concatenate([vals(cols), ones], axis=1))
            l = oe[:, HEAD_DIM:]
            o_ref[ri, q_rows, cols] = (oe[:, :HEAD_DIM] / l).astype(BF16)
            mx_tile = jnp.where(lane == h, mx, mx_tile)
            l_tile = jnp.where(lane == h, l, l_tile)
        lse_ref[ri, q_rows, :] = mx_tile * LN2 + jnp.log(l_tile)

    head = slice(0, blk_rows)
    for ri in range(q_ref.shape[0]):
        block(ri, head,
              lambda cols: jnp.concatenate([kp_ref[ri, :, cols], k_ref[ri, head, cols]], axis=0),
              lambda cols: jnp.concatenate([vp_ref[ri, :, cols], v_ref[ri, head, cols]], axis=0),
              first_band)
        for blk in range(1, nblk):
            q_rows = slice(blk * blk_rows, (blk + 1) * blk_rows)
            kv_rows = slice((blk - 1) * blk_rows, (blk + 1) * blk_rows)
            block(ri, q_rows, lambda cols: k_ref[ri, kv_rows, cols],
                  lambda cols: v_ref[ri, kv_rows, cols], band)


def _dilated_attention_group(qkv_g, g, rows):
    batch, d, sub_len, _ = qkv_g.shape
    hw = HEADS * HEAD_DIM
    res = max(1, min(d, rows // sub_len))
    rows = min(rows, sub_len)
    nblk = rows // ATTN_BLOCK

    def main_spec(kind):
        return pl.BlockSpec((None, res, rows, hw), lambda b, r, m: (b, r, m, kind))

    def prev_spec(kind):
        return pl.BlockSpec((None, res, ATTN_BLOCK, hw),
                            lambda b, r, m: (b, r, jnp.maximum(m * nblk - 1, 0), kind))

    return pl.pallas_call(
        functools.partial(_attn_kernel, nblk=nblk),
        out_shape=(jax.ShapeDtypeStruct((batch, d, sub_len, hw), BF16),
                   jax.ShapeDtypeStruct((batch, d, sub_len, LANES), F32)),
        grid=(batch, d // res, sub_len // rows),
        in_specs=[main_spec(0), main_spec(1), main_spec(2), prev_spec(1), prev_spec(2)],
        out_specs=(pl.BlockSpec((None, res, rows, hw), lambda b, r, m: (b, r, m, 0)),
                   pl.BlockSpec((None, res, rows, LANES), lambda b, r, m: (b, r, m, 0))),
        compiler_params=_params("parallel", "parallel", "arbitrary"),
        name=f"dilated_attn_g{g}",
    )(qkv_g, qkv_g, qkv_g, qkv_g, qkv_g)


def _merge_out_kernel(o0_ref, o1_ref, o2_ref, l0_ref, l1_ref, l2_ref, e_ref, x_ref, w_ref,
                      g_ref, b_ref, out_ref, on_ref, ln_ref):
    tm = x_ref.shape[0]
    outs, lses = [], []
    for gi, (o_ref, l_ref) in enumerate(((o0_ref, l0_ref), (o1_ref, l1_ref), (o2_ref, l2_ref))):
        d = o_ref.shape[0]
        n = tm // d
        if d == 1:
            outs.append(o_ref[0].astype(F32))
            lses.append(l_ref[0])
            continue
        for r in range(d):
            rows = _residue_rows(r, n, d)
            o_r = o_ref[r].astype(F32)
            for c in range(on_ref.shape[1]):
                on_ref[gi, c, rows, :] = o_r[:, c * LANES:(c + 1) * LANES]
            ln_ref[gi, rows, :] = l_ref[r]
        outs.append(jnp.concatenate([on_ref[gi, c] for c in range(on_ref.shape[1])], axis=1))
        lses.append(ln_ref[gi])
    mx = jnp.maximum(jnp.maximum(lses[0], lses[1]), lses[2])
    ws = [jnp.exp(l - mx) for l in lses]
    inv = 1.0 / (ws[0] + ws[1] + ws[2])
    e = e_ref[...]
    merged = None
    for w, o in zip(ws, outs):
        w = w * inv
        w_hi = w.astype(BF16)
        w_lo = (w - w_hi.astype(F32)).astype(BF16)
        w_full = _dot(jnp.concatenate([w_hi, w_lo], axis=1), e)
        term = w_full * o
        merged = term if merged is None else merged + term
    y = _dot(merged.astype(BF16), w_ref[...])
    r = DEEPNORM_ALPHA * x_ref[...] + y
    out_ref[...] = _layer_norm(r, g_ref[...], b_ref[...])


def _merge_out(os, lses, x2d, w_out_bf16, ln_g, ln_b, seq, tm):
    tokens, d_model = x2d.shape
    hw = HEADS * HEAD_DIM
    seq_tiles = seq // tm
    expand = (jnp.arange(2 * LANES)[:, None] % LANES == (jnp.arange(hw)[None, :] // HEAD_DIM)).astype(BF16)
    row = lambda i: (i, 0)
    fixed = lambda i: (0, 0)

    def group_spec(arr):
        d, width = arr.shape[1], arr.shape[3]
        return pl.BlockSpec((None, d, tm // d, width),
                            lambda i: (i // seq_tiles, 0, i % seq_tiles, 0))

    return pl.pallas_call(
        _merge_out_kernel,
        out_shape=jax.ShapeDtypeStruct((tokens, d_model), F32),
        grid=(tokens // tm,),
        in_specs=[group_spec(a) for a in os] + [group_spec(a) for a in lses] + [
            pl.BlockSpec((2 * LANES, hw), fixed),
            pl.BlockSpec((tm, d_model), row),
            pl.BlockSpec((hw, d_model), fixed),
            pl.BlockSpec((1, d_model), fixed),
            pl.BlockSpec((1, d_model), fixed),
        ],
        out_specs=pl.BlockSpec((tm, d_model), row),
        scratch_shapes=[pltpu.VMEM((N_GROUPS, hw // LANES, tm, LANES), F32),
                        pltpu.VMEM((N_GROUPS, tm, LANES), F32)],
        compiler_params=_params("parallel"),
        name="attn_merge_out_ln",
    )(*os, *lses, expand, x2d, w_out_bf16, ln_g.reshape(1, -1), ln_b.reshape(1, -1))


def _ffn_kernel(x_ref, halo_ref, wup_ref, cw_ref, cb_ref, wdown_ref, g_ref, b_ref, out_ref,
                xe_ref, h_ref, *, seq_tiles, d_ff, chunk, ln_split):
    i = pl.program_id(0)
    x = x_ref[...]
    first = (i % seq_tiles) == 0
    xe_ref[0:SUBLANES, :] = jnp.where(first, 0.0, halo_ref[...]).astype(BF16)
    xe_ref[SUBLANES:, :] = x.astype(BF16)
    xe = xe_ref[...]
    for c in range(d_ff // chunk):
        gcols = slice(c * chunk, (c + 1) * chunk)
        vcols = slice(d_ff + c * chunk, d_ff + (c + 1) * chunk)
        gate = _dot(xe, wup_ref[:, gcols])
        val = _dot(xe[SUBLANES:, :], wup_ref[:, vcols])
        conv = (cb_ref[:, gcols]
                + cw_ref[0:1, gcols] * pltpu.roll(gate, 2, axis=0)[SUBLANES:, :]
                + cw_ref[1:2, gcols] * pltpu.roll(gate, 1, axis=0)[SUBLANES:, :]
                + cw_ref[2:3, gcols] * gate[SUBLANES:, :])
        act = conv * (1.0 / (1.0 + jnp.exp(-conv)))
        h_ref[:, gcols] = (act * val).astype(BF16)
    part = x_ref.shape[0] // ln_split
    for s in range(ln_split):
        rows = slice(s * part, (s + 1) * part)
        f = _dot(h_ref[rows, :], wdown_ref[...])
        r = DEEPNORM_ALPHA * x_ref[rows, :] + f
        out_ref[rows, :] = _layer_norm(r, g_ref[...], b_ref[...])


def _conv_ffn(x2d, w_up_bf16, conv_w, conv_b, w_down_bf16, ln_g, ln_b, layer, seq, tm, chunk):
    tokens, d_model = x2d.shape
    d_ff = w_down_bf16.shape[1]
    seq_tiles = seq // tm
    halo_blocks = tm // SUBLANES
    fixed = lambda i: (0, 0)
    of_layer = lambda i: (layer, 0, 0)
    single = dict(pipeline_mode=pl.Buffered(1))
    return pl.pallas_call(
        functools.partial(_ffn_kernel, seq_tiles=seq_tiles, d_ff=d_ff, chunk=chunk, ln_split=4),
        out_shape=jax.ShapeDtypeStruct((tokens, d_model), F32),
        grid=(tokens // tm,),
        in_specs=[
            pl.BlockSpec((tm, d_model), lambda i: (i, 0)),
            pl.BlockSpec((SUBLANES, d_model), lambda i: (jnp.maximum(i * halo_blocks - 1, 0), 0)),
            pl.BlockSpec((None, d_model, 2 * d_ff), of_layer, **single),
            pl.BlockSpec((CONV_WIDTH, d_ff), fixed),
            pl.BlockSpec((1, d_ff), fixed),
            pl.BlockSpec((None, d_ff, d_model), of_layer, **single),
            pl.BlockSpec((1, d_model), fixed),
            pl.BlockSpec((1, d_model), fixed),
        ],
        out_specs=pl.BlockSpec((tm, d_model), lambda i: (i, 0)),
        scratch_shapes=[pltpu.VMEM((tm + SUBLANES, d_model), BF16),
                        pltpu.VMEM((tm, d_ff), BF16)],
        compiler_params=_params("parallel"),
        name="conv_ffn_ln",
    )(x2d, x2d, w_up_bf16, conv_w, conv_b.reshape(1, -1), w_down_bf16,
      ln_g.reshape(1, -1), ln_b.reshape(1, -1))


def _ssm_in_kernel(x_ref, wt_ref, ut_ref):
    gc = SSM_GROUP_CH
    u = _dot_nt(wt_ref[...], x_ref[...].astype(BF16))
    for g in range(ut_ref.shape[0]):
        for q in range(ut_ref.shape[1] // gc):
            ut_ref[g, q * gc:(q + 1) * gc, :] = u[g * gc:(g + 1) * gc, q * SSM_CHUNK:(q + 1) * SSM_CHUNK]


def _ssm_in(x2d, w_in_t_bf16, tn):
    tokens, d_model = x2d.shape
    d_ssm = w_in_t_bf16.shape[0]
    gc = SSM_GROUP_CH
    groups = d_ssm // gc
    return pl.pallas_call(
        _ssm_in_kernel,
        out_shape=jax.ShapeDtypeStruct((groups, tokens // SSM_CHUNK * gc, SSM_CHUNK), F32),
        grid=(tokens // tn,),
        in_specs=[pl.BlockSpec((tn, d_model), lambda i: (i, 0)),
                  pl.BlockSpec((d_ssm, d_model), lambda i: (0, 0))],
        out_specs=pl.BlockSpec((groups, tn // SSM_CHUNK * gc, SSM_CHUNK), lambda i: (0, i, 0)),
        compiler_params=_params("parallel"),
        name="ssm_in_proj",
    )(x2d, w_in_t_bf16)


def _split_bf16(x):
    hi = x.astype(BF16)
    return hi, (x - hi.astype(F32)).astype(BF16)


def _ssm_core_kernel(ut_ref, rpt_ref, fpt_ref, ba_ref, bb_ref, ca_ref, cb_ref, la_ref, lb_ref,
                     ct_ref, pw_ref, ctn_ref, ban_ref, bbn_ref, pwn_ref,
                     yt_ref, toep0_ref, toep1_ref, kv_ref, cbs_ref, pm_ref, cmt_ref, yacc_ref, st_ref,
                     tmp_ref, *, chunks_per_seq):
    gc = SSM_GROUP_CH
    L = SSM_CHUNK
    rows = ut_ref.shape[0] // gc
    n2 = 2 * SSM_STATE
    pair = 2 * L
    steps = gc // 2
    g = pl.program_id(0)
    slot = g % 2

    srow = lax.broadcasted_iota(jnp.int32, (L, L), 0)
    tcol = lax.broadcasted_iota(jnp.int32, (L, L), 1)
    causal = tcol >= srow
    conj = jnp.where(lax.broadcasted_iota(jnp.int32, (1, n2), 1) < SSM_STATE, 1.0, -1.0)

    def taps(c_ref, a_ref, b_ref, p_ref):
        ct = c_ref[...]
        ct_sw = pltpu.roll(ct, SSM_STATE, axis=1)
        for c in range(gc):
            cb_c = ct * a_ref[c:c + 1, :] + ct_sw * b_ref[c:c + 1, :]
            cbs_ref[c * gc:(c + 1) * gc, :] = cb_c * conj
        m_hi, m_lo = _split_bf16(cbs_ref[...])
        p_hi, p_lo = _split_bf16(p_ref[...])
        kv_ref[...] = _dot(m_hi, p_hi) + (_dot(m_hi, p_lo) + _dot(m_lo, p_hi))

    def build_pair(i, dst_ref):
        for cc in range(2):
            c = 2 * i + cc
            for cp in range(gc):
                kv = kv_ref[pl.ds(c * gc + cp, 1), :]
                blk = pltpu.roll(jnp.broadcast_to(kv, (L, L)), 0, 1, stride=1, stride_axis=0)
                blk = jnp.where(causal, blk, 0.0)
                dst_ref[pl.ds(pl.multiple_of(c * L, L), L), cp * L:(cp + 1) * L] = blk.astype(BF16)

    @pl.when(g == 0)
    def _():
        taps(ct_ref, ba_ref, bb_ref, pw_ref)

        def first(i, carry):
            build_pair(i, toep0_ref)
            return carry
        lax.fori_loop(0, steps, first, 0)

    taps(ctn_ref, ban_ref, bbn_ref, pwn_ref)

    quarter = rows * gc // 4
    for a in range(4):
        tmp_ref[a] = ut_ref[pl.ds(a, quarter, stride=4), :]

    rpt = rpt_ref[...]
    rpt_sw = pltpu.roll(rpt, SSM_STATE, axis=1)
    fpt = fpt_ref[...]
    fpt_sw = pltpu.roll(fpt, SSM_STATE, axis=1)
    st_ref[...] = jnp.zeros_like(st_ref)
    yacc_ref[...] = jnp.zeros_like(yacc_ref)

    def pipeline(cur_ref, nxt_ref):
        def step(i, carry):
            build_pair(i, nxt_ref)
            halves = []
            for cc in range(2):
                c = 2 * i + cc
                halves.append(tmp_ref[c % 4, pl.ds(c // 4, rows, stride=4), :].astype(BF16))
                crow = pl.ds(c, 1)
                pm_ref[cc * L:(cc + 1) * L, :] = (rpt * ba_ref[crow, :] + rpt_sw * bb_ref[crow, :]).astype(BF16)
                cmt_ref[pl.ds(pl.multiple_of(c * L, L), L), :] = (
                    fpt * ca_ref[crow, :] + fpt_sw * cb_ref[crow, :]).astype(BF16)
            lhs = jnp.concatenate(halves, axis=1)
            rws = pl.ds(pl.multiple_of(i * pair, pair), pair)
            yacc_ref[...] += _dot(lhs, cur_ref[rws, :])
            st_ref[...] += _dot(lhs, pm_ref[...])
            return carry
        lax.fori_loop(0, steps, step, 0)

    @pl.when(slot == 0)
    def _():
        pipeline(toep0_ref, toep1_ref)

    @pl.when(slot == 1)
    def _():
        pipeline(toep1_ref, toep0_ref)

    st = st_ref[...]
    a = la_ref[...]
    b = lb_ref[...]
    jrow = lax.broadcasted_iota(jnp.int32, (rows, n2), 0) % chunks_per_seq
    sh = 1
    while sh < chunks_per_seq:
        prev = jnp.where(jrow >= sh, pltpu.roll(st, sh, axis=0), 0.0)
        st = st + prev * a + pltpu.roll(prev, SSM_STATE, axis=1) * b
        a, b = a * a - b * b, 2.0 * a * b
        sh *= 2
    carried = jnp.where(jrow >= 1, pltpu.roll(st, 1, axis=0), 0.0)
    yacc_ref[...] += _dot_nt(carried.astype(BF16), cmt_ref[...])

    for cp in range(gc):
        tmp_ref[cp % 4, pl.ds(cp // 4, rows, stride=4), :] = yacc_ref[:, cp * L:(cp + 1) * L]
    for a4 in range(4):
        yt_ref[pl.ds(a4, quarter, stride=4), :] = tmp_ref[a4]


def _ssm_core(ut, rpt, fpt, ba, bb, ca, cb, lam_a, lam_b, ctile, pw, chunks_per_seq):
    groups, group_rows, L = ut.shape
    gc, n2 = SSM_GROUP_CH, 2 * SSM_STATE
    rows = group_rows // gc
    per_group = lambda g: (g, 0, 0)
    next_group = lambda g: (jnp.minimum(g + 1, groups - 1), 0, 0)
    return pl.pallas_call(
        functools.partial(_ssm_core_kernel, chunks_per_seq=chunks_per_seq),
        out_shape=jax.ShapeDtypeStruct((groups, group_rows, L), F32),
        grid=(groups,),
        in_specs=[
            pl.BlockSpec((None, group_rows, L), per_group),
            pl.BlockSpec((None, L, n2), per_group),
            pl.BlockSpec((None, L, n2), per_group),
            pl.BlockSpec((None, gc, n2), per_group),
            pl.BlockSpec((None, gc, n2), per_group),
            pl.BlockSpec((None, gc, n2), per_group),
            pl.BlockSpec((None, gc, n2), per_group),
            pl.BlockSpec((None, 1, n2), per_group),
            pl.BlockSpec((None, 1, n2), per_group),
            pl.BlockSpec((None, gc, n2), per_group),
            pl.BlockSpec((None, n2, L), per_group),
            pl.BlockSpec((None, gc, n2), next_group),
            pl.BlockSpec((None, gc, n2), next_group),
            pl.BlockSpec((None, gc, n2), next_group),
            pl.BlockSpec((None, n2, L), next_group),
        ],
        out_specs=pl.BlockSpec((None, group_rows, L), per_group),
        scratch_shapes=[pltpu.VMEM((gc * L, gc * L), BF16),
                        pltpu.VMEM((gc * L, gc * L), BF16),
                        pltpu.VMEM((gc * gc, L), F32),
                        pltpu.VMEM((gc * gc, n2), F32),
                        pltpu.VMEM((2 * L, n2), BF16),
                        pltpu.VMEM((gc * L, n2), BF16),
                        pltpu.VMEM((rows, gc * L), F32),
                        pltpu.VMEM((rows, n2), F32),
                        pltpu.VMEM((4, rows * gc // 4, L), F32)],
        compiler_params=_params("arbitrary"),
        name="ssm_chunk_conv",
    )(ut, rpt, fpt, ba, bb, ca, cb, lam_a, lam_b, ctile, pw, ctile, ba, bb, pw)


def _ssm_operators(a_re, a_im, log_dt, b_re, b_im, c_re, c_im):
    L = SSM_CHUNK
    a_re = a_re.astype(F32)
    a_im = a_im.astype(F32)
    dt = jnp.exp(log_dt.astype(F32))[:, None]
    mag = jnp.exp(a_re * dt)
    lam_re = mag * jnp.cos(a_im * dt)
    lam_im = mag * jnp.sin(a_im * dt)
    nr, ni = lam_re - 1.0, lam_im
    den = a_re * a_re + a_im * a_im
    coef_re = ((nr * a_re + ni * a_im) / den)[..., None]
    coef_im = ((ni * a_re - nr * a_im) / den)[..., None]
    b_re = b_re.astype(F32)
    b_im = b_im.astype(F32)
    bb_re = coef_re * b_re - coef_im * b_im
    bb_im = coef_re * b_im + coef_im * b_re
    c_re = c_re.astype(F32)
    c_im = c_im.astype(F32)

    k = jnp.arange(L + 1, dtype=F32)
    pmag = jnp.exp((a_re * dt)[..., None] * k)
    ang = (a_im * dt)[..., None] * k
    pw_re = pmag * jnp.cos(ang)
    pw_im = pmag * jnp.sin(ang)

    def lanes(re, im):
        return jnp.concatenate([jnp.swapaxes(re, 1, 2), jnp.swapaxes(im, 1, 2)], axis=-1)

    rpt = lanes(jnp.flip(pw_re[..., :L], axis=-1), jnp.flip(pw_im[..., :L], axis=-1))
    fpt = lanes(pw_re[..., 1:], pw_im[..., 1:])
    ba = lanes(bb_re, bb_re)
    bb = lanes(-bb_im, bb_im)
    ca = jnp.concatenate([c_re, -c_re], axis=-1)
    cb = jnp.concatenate([-c_im, -c_im], axis=-1)
    lam_a = jnp.concatenate([pw_re[..., L], pw_re[..., L]], axis=-1)[:, None, :]
    lam_b = jnp.concatenate([-pw_im[..., L], pw_im[..., L]], axis=-1)[:, None, :]
    ctile = jnp.concatenate([c_re, c_im], axis=-1)
    pw = jnp.concatenate([pw_re[..., :L], pw_im[..., :L]], axis=1)
    return rpt, fpt, ba, bb, ca, cb, lam_a, lam_b, ctile, pw


def _ssm_out_kernel(yt_ref, ut_ref, d_ref, wg_ref, bg_ref, wo_ref, x_ref, g_ref, b_ref, out_ref, *, sub):
    gc = SSM_GROUP_CH
    groups, nq = yt_ref.shape[0], yt_ref.shape[1] // gc
    qs = sub // SSM_CHUNK
    n_sub = nq // qs

    def channel_major(ref, q0):
        return jnp.concatenate(
            [jnp.concatenate([ref[g, q * gc:(q + 1) * gc, :] for g in range(groups)], axis=0)
             for q in range(q0, q0 + qs)], axis=1)

    def gelu_gate(s):
        y = channel_major(yt_ref, s * qs) + d_ref[...] * channel_major(ut_ref, s * qs)
        z = 0.5 * y * (1.0 + jnp.tanh(math.sqrt(2.0 / math.pi) * (y + 0.044715 * (y * y * y))))
        return z, _dot(wg_ref[...], z.astype(BF16)) + bg_ref[...]

    def glu_out(s, z, gate):
        rows = slice(s * sub, (s + 1) * sub)
        zs = z * (1.0 / (1.0 + jnp.exp(-gate)))
        f = _dot_tn(zs.astype(BF16), wo_ref[...])
        r = DEEPNORM_ALPHA * x_ref[rows, :] + f
        out_ref[rows, :] = _layer_norm(r, g_ref[...], b_ref[...])

    pending = gelu_gate(0)
    for s in range(n_sub):
        upcoming = gelu_gate(s + 1) if s + 1 < n_sub else None
        glu_out(s, *pending)
        pending = upcoming


def _ssm_out(yt, ut, d_skip, w_glu_t_bf16, b_glu, w_out_bf16, x2d, ln_g, ln_b, tn, sub):
    tokens, d_model = x2d.shape
    groups = yt.shape[0]
    d_ssm = groups * SSM_GROUP_CH
    fixed = lambda i: (0, 0)
    chunked = pl.BlockSpec((groups, tn // SSM_CHUNK * SSM_GROUP_CH, SSM_CHUNK), lambda i: (0, i, 0))
    return pl.pallas_call(
        functools.partial(_ssm_out_kernel, sub=sub),
        out_shape=jax.ShapeDtypeStruct((tokens, d_model), F32),
        grid=(tokens // tn,),
        in_specs=[
            chunked,
            chunked,
            pl.BlockSpec((d_ssm, 1), fixed),
            pl.BlockSpec((d_ssm, d_ssm), fixed),
            pl.BlockSpec((d_ssm, 1), fixed),
            pl.BlockSpec((d_ssm, d_model), fixed),
            pl.BlockSpec((tn, d_model), lambda i: (i, 0)),
            pl.BlockSpec((1, d_model), fixed),
            pl.BlockSpec((1, d_model), fixed),
        ],
        out_specs=pl.BlockSpec((tn, d_model), lambda i: (i, 0)),
        compiler_params=_params("parallel"),
        name="ssm_glu_out_ln",
    )(yt, ut, d_skip.reshape(-1, 1), w_glu_t_bf16, b_glu.reshape(-1, 1), w_out_bf16, x2d,
      ln_g.reshape(1, -1), ln_b.reshape(1, -1))


def _rope_tables(seq):
    inv_freq = ROPE_THETA ** (-jnp.arange(0, HEAD_DIM, 2, dtype=F32) / HEAD_DIM)
    inv_freq = jnp.concatenate([inv_freq, inv_freq])[None, :]
    ang_a = (jnp.arange(seq // ROPE_BLOCK, dtype=F32) * ROPE_BLOCK)[:, None] * inv_freq
    ang_b = jnp.arange(ROPE_BLOCK, dtype=F32)[:, None] * inv_freq
    sign = jnp.where(jnp.arange(HEAD_DIM) < HEAD_DIM // 2, -1.0, 1.0).astype(F32)[None, :]
    cb, sb = jnp.cos(ang_b), jnp.sin(ang_b)
    return jnp.cos(ang_a), jnp.sin(ang_a), cb, sb, cb * sign, sb * sign


def _row_tile(seq, want):
    tm = min(want, seq)
    assert seq % tm == 0
    return tm


def kernel(x, attn_w_in, attn_w_out, ssm_w_in, ssm_a_re, ssm_a_im, ssm_log_dt, ssm_b_re, ssm_b_im,
           ssm_c_re, ssm_c_im, ssm_d, ssm_w_glu, ssm_b_glu, ssm_w_out, ffn_w_up, ffn_conv_w,
           ffn_conv_b, ffn_w_down, ln_g, ln_b):
    batch, seq, d_model = x.shape
    assert seq % DILATION_PAIRS[-1][0] == 0 and seq % SSM_CHUNK == 0
    assert all(w // d == ATTN_BLOCK for w, d in DILATION_PAIRS)
    h = x.reshape(batch * seq, d_model)
    ffn_chunk = 256

    w_up_all = ffn_w_up.astype(BF16)
    w_down_all = ffn_w_down.astype(BF16)

    def ffn(h, i):
        return _conv_ffn(h, w_up_all, ffn_conv_w[i], ffn_conv_b[i], w_down_all, ln_g[i, 1], ln_b[i, 1],
                         i, seq, _row_tile(seq, 1024), ffn_chunk)

    rope = _rope_tables(seq)
    w_in = attn_w_in[0].astype(BF16)
    os, lses = [], []
    for g, (_, dilation) in enumerate(DILATION_PAIRS):
        qkv_g = _qkv_rope(h, w_in, rope, g, dilation, batch, seq, _row_tile(seq, 1024))
        o, lse = _dilated_attention_group(qkv_g, g, 2048)
        os.append(o)
        lses.append(lse)
    h = _merge_out(os, lses, h, attn_w_out[0].astype(BF16), ln_g[0, 0], ln_b[0, 0], seq,
                   _row_tile(seq, 512))
    h = ffn(h, 0)

    ssm_ops = _ssm_operators(
        ssm_a_re[0], ssm_a_im[0], ssm_log_dt[0], ssm_b_re[0], ssm_b_im[0], ssm_c_re[0], ssm_c_im[0])
    ut = _ssm_in(h, ssm_w_in[0].T.astype(BF16), _row_tile(seq, 1024))
    yt = _ssm_core(ut, *ssm_ops, seq // SSM_CHUNK)
    h = _ssm_out(yt, ut, ssm_d[0], ssm_w_glu[0].T.astype(BF16), ssm_b_glu[0],
                 ssm_w_out[0].astype(BF16), h, ln_g[1, 0], ln_b[1, 0], _row_tile(seq, 1024), 512)
    h = ffn(h, 1)
    return h.reshape(batch, seq, d_model)
```

```python
import functools
import math

import jax
import jax.numpy as jnp
from jax import lax
from jax.experimental import pallas as pl
from jax.experimental.pallas import tpu as pltpu

F32 = jnp.float32
BF16 = jnp.bfloat16

DEPTH = 2
DILATION_PAIRS = ((128, 1), (512, 4), (2048, 16))
N_GROUPS = len(DILATION_PAIRS)
HEADS = 8
HEAD_DIM = 128
ROPE_THETA = 10000.0
SSM_GROUP_CH = 16
SSM_STATE = 64
CONV_WIDTH = 3
DEEPNORM_ALPHA = (2.0 * DEPTH) ** 0.25
LN_EPS = 1e-5
NEG_INF = -1e30
LOG2E = math.log2(math.e)
LN2 = math.log(2.0)

LANES = 128
SUBLANES = 8
VMEM_LIMIT_BYTES = 56 * 1024 * 1024

ATTN_BLOCK = 128
ROPE_BLOCK = 128
SSM_CHUNK = 128


def _params(*semantics):
    return pltpu.CompilerParams(dimension_semantics=semantics,
                                vmem_limit_bytes=VMEM_LIMIT_BYTES)


def _layer_norm(r, g, b):
    mu = jnp.mean(r, axis=-1, keepdims=True)
    d = r - mu
    var = jnp.mean(d * d, axis=-1, keepdims=True)
    return d * lax.rsqrt(var + LN_EPS) * g + b


def _dot(a, b):
    return jnp.dot(a, b, preferred_element_type=F32)


def _dot_nt(a, b):
    return lax.dot_general(a, b, (((1,), (1,)), ((), ())), preferred_element_type=F32)


def _dot_tn(a, b):
    return lax.dot_general(a, b, (((0,), (0,)), ((), ())), preferred_element_type=F32)


def _residue_rows(r, n, d):
    return pl.ds(r, n, stride=d) if d > 1 else slice(None)


def _qkv_kernel(*refs, d):
    n_slabs = len(refs) - 14
    x_refs = refs[:n_slabs]
    (w_ref, ca_ref, sa_ref, cb_ref, sb_ref, cbs_ref, sbs_ref, o_ref,
     xb_ref, cos_ref, sin_ref, cs_ref, sn_ref, tmp_ref) = refs[n_slabs:]
    tm = cos_ref.shape[0]
    n = tm // d
    hw = HEADS * HEAD_DIM

    for q in range(tm // ROPE_BLOCK):
        blk = slice(q * ROPE_BLOCK, (q + 1) * ROPE_BLOCK)
        ca = ca_ref[q:q + 1, :]
        sa = sa_ref[q:q + 1, :]
        cos_ref[blk, :] = ca * cb_ref[...] - sa * sb_ref[...]
        sin_ref[blk, :] = sa * cbs_ref[...] + ca * sbs_ref[...]

    def residues(src_ref):
        if d % 8:
            return [src_ref[_residue_rows(r, n, d), :] for r in range(d)]
        for a in range(4):
            tmp_ref[a] = src_ref[pl.ds(a, tm // 4, stride=4), :]
        return [tmp_ref[r % 4, pl.ds(r // 4, n, stride=d // 4), :] for r in range(d)]

    for c, x_ref in enumerate(x_refs):
        for r, piece in enumerate(residues(x_ref)):
            xb_ref[r * n:(r + 1) * n, c * LANES:(c + 1) * LANES] = piece.astype(BF16)
    for src_ref, dst_ref in ((cos_ref, cs_ref), (sin_ref, sn_ref)):
        for r, piece in enumerate(residues(src_ref)):
            dst_ref[r * n:(r + 1) * n, :] = piece
    xb = xb_ref[...]

    for kind in range(3):
        acc = _dot(xb, w_ref[:, kind * hw:(kind + 1) * hw])
        if kind == 2:
            for r in range(d):
                o_ref[r, :, kind * hw:(kind + 1) * hw] = acc[r * n:(r + 1) * n, :].astype(BF16)
            continue
        scale = HEAD_DIM ** -0.5 * LOG2E if kind == 0 else 1.0
        cos = cs_ref[...] * scale
        sin = sn_ref[...] * scale
        for h in range(HEADS):
            t = acc[:, h * HEAD_DIM:(h + 1) * HEAD_DIM]
            rot = (t * cos + pltpu.roll(t, HEAD_DIM // 2, axis=1) * sin).astype(BF16)
            lanes = slice(kind * hw + h * HEAD_DIM, kind * hw + (h + 1) * HEAD_DIM)
            for r in range(d):
                o_ref[r, :, lanes] = rot[r * n:(r + 1) * n, :]


def _qkv_rope(x2d, w_in_bf16, rope, g, d, batch, seq, tm):
    d_model = x2d.shape[1]
    hw = HEADS * HEAD_DIM
    seq_tiles = seq // tm
    n = tm // d
    coarse = pl.BlockSpec((tm // ROPE_BLOCK, HEAD_DIM), lambda i: (i % seq_tiles, 0))
    fine = pl.BlockSpec((ROPE_BLOCK, HEAD_DIM), lambda i: (0, 0))
    return pl.pallas_call(
        functools.partial(_qkv_kernel, d=d),
        out_shape=jax.ShapeDtypeStruct((batch, d, seq // d, 3 * hw), BF16),
        grid=(batch * seq_tiles,),
        in_specs=[pl.BlockSpec((tm, LANES), lambda i, c=c: (i, c))
                  for c in range(d_model // LANES)] + [
            pl.BlockSpec((d_model, 3 * hw), lambda i: (0, g)),
            coarse, coarse, fine, fine, fine, fine,
        ],
        out_specs=pl.BlockSpec((None, d, n, 3 * hw),
                               lambda i: (i // seq_tiles, 0, i % seq_tiles, 0)),
        scratch_shapes=[pltpu.VMEM((tm, d_model), BF16),
                        pltpu.VMEM((tm, HEAD_DIM), F32),
                        pltpu.VMEM((tm, HEAD_DIM), F32),
                        pltpu.VMEM((tm, HEAD_DIM), F32),
                        pltpu.VMEM((tm, HEAD_DIM), F32),
                        pltpu.VMEM((4, tm // 4, LANES), F32)],
        compiler_params=_params("parallel"),
        name=f"qkv_rope_g{g}",
    )(*([x2d] * (d_model // LANES)), w_in_bf16, *rope)


def _attn_kernel(q_ref, k_ref, v_ref, kp_ref, vp_ref, o_ref, lse_ref, *, nblk):
    m = pl.program_id(2)
    blk_rows = ATTN_BLOCK
    row = lax.broadcasted_iota(jnp.int32, (blk_rows, 2 * blk_rows), 0)
    col = lax.broadcasted_iota(jnp.int32, (blk_rows, 2 * blk_rows), 1)
    band = jnp.logical_and(col >= row, col <= row + blk_rows)
    first_band = jnp.logical_and(band, jnp.logical_or(col >= blk_rows, m > 0))
    lane = lax.broadcasted_iota(jnp.int32, (blk_rows, LANES), 1)
    ones = jnp.ones((2 * blk_rows, HEAD_DIM), BF16)

    def block(ri, q_rows, keys, vals, mask):
        mx_tile = jnp.zeros((blk_rows, LANES), F32)
        l_tile = jnp.ones((blk_rows, LANES), F32)
        for h in range(HEADS):
            cols = slice(h * HEAD_DIM, (h + 1) * HEAD_DIM)
            s = jnp.where(mask, _dot_nt(q_ref[ri, q_rows, cols], keys(cols)), NEG_INF)
            mx = jnp.max(jnp.maximum(s[:, :blk_rows], s[:, blk_rows:]), axis=1, keepdims=True)
            p = jnp.exp2(s - mx).astype(BF16)
            oe = _dot(p, jnp.concatenate([vals(cols), ones], axis=1))
            l = oe[:, HEAD_DIM:]
            o_ref[ri, q_rows, cols] = (oe[:, :HEAD_DIM] / l).astype(BF16)
            mx_tile = jnp.where(lane == h, mx, mx_tile)
            l_tile = jnp.where(lane == h, l, l_tile)
        lse_ref[ri, q_rows, :] = mx_tile * LN2 + jnp.log(l_tile)

    head = slice(0, blk_rows)
    for ri in range(q_ref.shape[0]):
        block(ri, head,
              lambda cols: jnp.concatenate([kp_ref[ri, :, cols], k_ref[ri, head, cols]], axis=0),
              lambda cols: jnp.concatenate([vp_ref[ri, :, cols], v_ref[ri, head, cols]], axis=0),
              first_band)
        for blk in range(1, nblk):
            q_rows = slice(blk * blk_rows, (blk + 1) * blk_rows)
            kv_rows = slice((blk - 1) * blk_rows, (blk + 1) * blk_rows)
            block(ri, q_rows, lambda cols: k_ref[ri, kv_rows, cols],
                  lambda cols: v_ref[ri, kv_rows, cols], band)


def _dilated_attention_group(qkv_g, g, rows):
    batch, d, sub_len, _ = qkv_g.shape
    hw = HEADS * HEAD_DIM
    res = max(1, min(d, rows // sub_len))
    rows = min(rows, sub_len)
    nblk = rows // ATTN_BLOCK

    def main_spec(kind):
        return pl.BlockSpec((None, res, rows, hw), lambda b, r, m: (b, r, m, kind))

    def prev_spec(kind):
        return pl.BlockSpec((None, res, ATTN_BLOCK, hw),
                            lambda b, r, m: (b, r, jnp.maximum(m * nblk - 1, 0), kind))

    return pl.pallas_call(
        functools.partial(_attn_kernel, nblk=nblk),
        out_shape=(jax.ShapeDtypeStruct((batch, d, sub_len, hw), BF16),
                   jax.ShapeDtypeStruct((batch, d, sub_len, LANES), F32)),
        grid=(batch, d // res, sub_len // rows),
        in_specs=[main_spec(0), main_spec(1), main_spec(2), prev_spec(1), prev_spec(2)],
        out_specs=(pl.BlockSpec((None, res, rows, hw), lambda b, r, m: (b, r, m, 0)),
                   pl.BlockSpec((None, res, rows, LANES), lambda b, r, m: (b, r, m, 0))),
        compiler_params=_params("parallel", "parallel", "arbitrary"),
        name=f"dilated_attn_g{g}",
    )(qkv_g, qkv_g, qkv_g, qkv_g, qkv_g)


def _merge_out_kernel(o0_ref, o1_ref, o2_ref, l0_ref, l1_ref, l2_ref, e_ref, x_ref, w_ref,
                      g_ref, b_ref, out_ref, on_ref, ln_ref):
    tm = x_ref.shape[0]
    outs, lses = [], []
    for gi, (o_ref, l_ref) in enumerate(((o0_ref, l0_ref), (o1_ref, l1_ref), (o2_ref, l2_ref))):
        d = o_ref.shape[0]
        n = tm // d
        if d == 1:
            outs.append(o_ref[0].astype(F32))
            lses.append(l_ref[0])
            continue
        for r in range(d):
            rows = _residue_rows(r, n, d)
            o_r = o_ref[r].astype(F32)
            for c in range(on_ref.shape[1]):
                on_ref[gi, c, rows, :] = o_r[:, c * LANES:(c + 1) * LANES]
            ln_ref[gi, rows, :] = l_ref[r]
        outs.append(jnp.concatenate([on_ref[gi, c] for c in range(on_ref.shape[1])], axis=1))
        lses.append(ln_ref[gi])
    mx = jnp.maximum(jnp.maximum(lses[0], lses[1]), lses[2])
    ws = [jnp.exp(l - mx) for l in lses]
    inv = 1.0 / (ws[0] + ws[1] + ws[2])
    e = e_ref[...]
    merged = None
    for w, o in zip(ws, outs):
        w = w * inv
        w_hi = w.astype(BF16)
        w_lo = (w - w_hi.astype(F32)).astype(BF16)
        w_full = _dot(jnp.concatenate([w_hi, w_lo], axis=1), e)
        term = w_full * o
        merged = term if merged is None else merged + term
    y = _dot(merged.astype(BF16), w_ref[...])
    r = DEEPNORM_ALPHA * x_ref[...] + y
    out_ref[...] = _layer_norm(r, g_ref[...], b_ref[...])


def _merge_out(os, lses, x2d, w_out_bf16, ln_g, ln_b, seq, tm):
    tokens, d_model = x2d.shape
    hw = HEADS * HEAD_DIM
    seq_tiles = seq // tm
    expand = (jnp.arange(2 * LANES)[:, None] % LANES == (jnp.arange(hw)[None, :] // HEAD_DIM)).astype(BF16)
    row = lambda i: (i, 0)
    fixed = lambda i: (0, 0)

    def group_spec(arr):
        d, width = arr.shape[1], arr.shape[3]
        return pl.BlockSpec((None, d, tm // d, width),
                            lambda i: (i // seq_tiles, 0, i % seq_tiles, 0))

    return pl.pallas_call(
        _merge_out_kernel,
        out_shape=jax.ShapeDtypeStruct((tokens, d_model), F32),
        grid=(tokens // tm,),
        in_specs=[group_spec(a) for a in os] + [group_spec(a) for a in lses] + [
            pl.BlockSpec((2 * LANES, hw), fixed),
            pl.BlockSpec((tm, d_model), row),
            pl.BlockSpec((hw, d_model), fixed),
            pl.BlockSpec((1, d_model), fixed),
            pl.BlockSpec((1, d_model), fixed),
        ],
        out_specs=pl.BlockSpec((tm, d_model), row),
        scratch_shapes=[pltpu.VMEM((N_GROUPS, hw // LANES, tm, LANES), F32),
                        pltpu.VMEM((N_GROUPS, tm, LANES), F32)],
        compiler_params=_params("parallel"),
        name="attn_merge_out_ln",
    )(*os, *lses, expand, x2d, w_out_bf16, ln_g.reshape(1, -1), ln_b.reshape(1, -1))


def _ffn_kernel(x_ref, halo_ref, wup_ref, cw_ref, cb_ref, wdown_ref, g_ref, b_ref, out_ref, *rest,
                seq_tiles, d_ff, chunk, ln_split):
    outb_ref = rest[0] if len(rest) == 3 else None
    xe_ref, h_ref = rest[-2:]
    i = pl.program_id(0)
    x = x_ref[...]
    first = (i % seq_tiles) == 0
    xe_ref[0:SUBLANES, :] = jnp.where(first, 0.0, halo_ref[...]).astype(BF16)
    xe_ref[SUBLANES:, :] = x.astype(BF16)
    xe = xe_ref[...]
    for c in range(d_ff // chunk):
        gcols = slice(c * chunk, (c + 1) * chunk)
        vcols = slice(d_ff + c * chunk, d_ff + (c + 1) * chunk)
        gate = _dot(xe, wup_ref[:, gcols])
        val = _dot(xe[SUBLANES:, :], wup_ref[:, vcols])
        conv = (cb_ref[:, gcols]
                + cw_ref[0:1, gcols] * pltpu.roll(gate, 2, axis=0)[SUBLANES:, :]
                + cw_ref[1:2, gcols] * pltpu.roll(gate, 1, axis=0)[SUBLANES:, :]
                + cw_ref[2:3, gcols] * gate[SUBLANES:, :])
        act = conv * (1.0 / (1.0 + jnp.exp(-conv)))
        h_ref[:, gcols] = (act * val).astype(BF16)
    part = x_ref.shape[0] // ln_split
    for s in range(ln_split):
        rows = slice(s * part, (s + 1) * part)
        f = _dot(h_ref[rows, :], wdown_ref[...])
        r = DEEPNORM_ALPHA * x_ref[rows, :] + f
        y = _layer_norm(r, g_ref[...], b_ref[...])
        out_ref[rows, :] = y
        if outb_ref is not None:
            outb_ref[rows, :] = y.astype(BF16)


def _conv_ffn(x2d, w_up_bf16, conv_w, conv_b, w_down_bf16, ln_g, ln_b, layer, seq, tm, chunk,
              with_bf16_copy=False):
    tokens, d_model = x2d.shape
    d_ff = w_down_bf16.shape[1]
    seq_tiles = seq // tm
    halo_blocks = tm // SUBLANES
    fixed = lambda i: (0, 0)
    of_layer = lambda i: (layer, 0, 0)
    single = dict(pipeline_mode=pl.Buffered(1))
    in_specs = [
        pl.BlockSpec((tm, d_model), lambda i: (i, 0)),
        pl.BlockSpec((SUBLANES, d_model), lambda i: (jnp.maximum(i * halo_blocks - 1, 0), 0)),
        pl.BlockSpec((None, d_model, 2 * d_ff), of_layer, **single),
        pl.BlockSpec((CONV_WIDTH, d_ff), fixed),
        pl.BlockSpec((1, d_ff), fixed),
        pl.BlockSpec((None, d_ff, d_model), of_layer, **single),
        pl.BlockSpec((1, d_model), fixed),
        pl.BlockSpec((1, d_model), fixed),
    ]
    operands = [x2d, x2d, w_up_bf16, conv_w, conv_b.reshape(1, -1), w_down_bf16,
                ln_g.reshape(1, -1), ln_b.reshape(1, -1)]
    out_shape = jax.ShapeDtypeStruct((tokens, d_model), F32)
    out_specs = pl.BlockSpec((tm, d_model), lambda i: (i, 0))
    if with_bf16_copy:
        out_shape = (out_shape, jax.ShapeDtypeStruct((tokens, d_model), BF16))
        out_specs = (out_specs, pl.BlockSpec((tm, d_model), lambda i: (i, 0)))
    return pl.pallas_call(
        functools.partial(_ffn_kernel, seq_tiles=seq_tiles, d_ff=d_ff, chunk=chunk, ln_split=4),
        out_shape=out_shape,
        grid=(tokens // tm,),
        in_specs=in_specs,
        out_specs=out_specs,
        scratch_shapes=[pltpu.VMEM((tm + SUBLANES, d_model), BF16),
                        pltpu.VMEM((tm, d_ff), BF16)],
        compiler_params=_params("parallel"),
        name="conv_ffn_ln",
    )(*operands)


def _ssm_in_kernel(x_ref, wt_ref, ut_ref):
    gc = SSM_GROUP_CH
    u = _dot_nt(wt_ref[...], x_ref[...])
    for g in range(ut_ref.shape[0]):
        for q in range(ut_ref.shape[1] // gc):
            ut_ref[g, q * gc:(q + 1) * gc, :] = u[g * gc:(g + 1) * gc, q * SSM_CHUNK:(q + 1) * SSM_CHUNK]


def _ssm_in(x2d_bf16, w_in_t_bf16, tn):
    tokens, d_model = x2d_bf16.shape
    d_ssm = w_in_t_bf16.shape[0]
    gc = SSM_GROUP_CH
    groups = d_ssm // gc
    return pl.pallas_call(
        _ssm_in_kernel,
        out_shape=jax.ShapeDtypeStruct((groups, tokens // SSM_CHUNK * gc, SSM_CHUNK), F32),
        grid=(tokens // tn,),
        in_specs=[pl.BlockSpec((tn, d_model), lambda i: (i, 0)),
                  pl.BlockSpec((d_ssm, d_model), lambda i: (0, 0))],
        out_specs=pl.BlockSpec((groups, tn // SSM_CHUNK * gc, SSM_CHUNK), lambda i: (0, i, 0)),
        compiler_params=_params("parallel"),
        name="ssm_in_proj",
    )(x2d_bf16, w_in_t_bf16)


def _split_bf16(x):
    hi = x.astype(BF16)
    return hi, (x - hi.astype(F32)).astype(BF16)


def _ssm_core_kernel(ut_ref, rpt_ref, fpt_ref, ba_ref, bb_ref, ca_ref, cb_ref, la_ref, lb_ref,
                     ct_ref, pw_ref, ctn_ref, ban_ref, bbn_ref, pwn_ref,
                     yt_ref, toep0_ref, toep1_ref, kv_ref, cbs_ref, pm_ref, cmt_ref, yacc_ref, st_ref,
                     tmp_ref, *, chunks_per_seq):
    gc = SSM_GROUP_CH
    L = SSM_CHUNK
    rows = ut_ref.shape[0] // gc
    n2 = 2 * SSM_STATE
    pair = 2 * L
    steps = gc // 2
    g = pl.program_id(0)
    slot = g % 2

    srow = lax.broadcasted_iota(jnp.int32, (L, L), 0)
    tcol = lax.broadcasted_iota(jnp.int32, (L, L), 1)
    causal = tcol >= srow
    conj = jnp.where(lax.broadcasted_iota(jnp.int32, (1, n2), 1) < SSM_STATE, 1.0, -1.0)

    def taps(c_ref, a_ref, b_ref, p_ref):
        ct = c_ref[...]
        ct_sw = pltpu.roll(ct, SSM_STATE, axis=1)
        for c in range(gc):
            cb_c = ct * a_ref[c:c + 1, :] + ct_sw * b_ref[c:c + 1, :]
            cbs_ref[c * gc:(c + 1) * gc, :] = cb_c * conj
        m_hi, m_lo = _split_bf16(cbs_ref[...])
        p_hi, p_lo = _split_bf16(p_ref[...])
        kv_ref[...] = _dot(m_hi, p_hi) + (_dot(m_hi, p_lo) + _dot(m_lo, p_hi))

    def build_pair(i, dst_ref):
        for cc in range(2):
            c = 2 * i + cc
            for cp in range(gc):
                kv = kv_ref[pl.ds(c * gc + cp, 1), :]
                blk = pltpu.roll(jnp.broadcast_to(kv, (L, L)), 0, 1, stride=1, stride_axis=0)
                blk = jnp.where(causal, blk, 0.0)
                dst_ref[pl.ds(pl.multiple_of(c * L, L), L), cp * L:(cp + 1) * L] = blk.astype(BF16)

    @pl.when(g == 0)
    def _():
        taps(ct_ref, ba_ref, bb_ref, pw_ref)

        def first(i, carry):
            build_pair(i, toep0_ref)
            return carry
        lax.fori_loop(0, steps, first, 0)

    taps(ctn_ref, ban_ref, bbn_ref, pwn_ref)

    quarter = rows * gc // 4
    for a in range(4):
        tmp_ref[a] = ut_ref[pl.ds(a, quarter, stride=4), :]

    rpt = rpt_ref[...]
    rpt_sw = pltpu.roll(rpt, SSM_STATE, axis=1)
    fpt = fpt_ref[...]
    fpt_sw = pltpu.roll(fpt, SSM_STATE, axis=1)
    st_ref[...] = jnp.zeros_like(st_ref)
    yacc_ref[...] = jnp.zeros_like(yacc_ref)

    def pipeline(cur_ref, nxt_ref):
        def step(i, carry):
            build_pair(i, nxt_ref)
            halves = []
            for cc in range(2):
                c = 2 * i + cc
                halves.append(tmp_ref[c % 4, pl.ds(c // 4, rows, stride=4), :].astype(BF16))
                crow = pl.ds(c, 1)
                pm_ref[cc * L:(cc + 1) * L, :] = (rpt * ba_ref[crow, :] + rpt_sw * bb_ref[crow, :]).astype(BF16)
                cmt_ref[pl.ds(pl.multiple_of(c * L, L), L), :] = (
                    fpt * ca_ref[crow, :] + fpt_sw * cb_ref[crow, :]).astype(BF16)
            lhs = jnp.concatenate(halves, axis=1)
            rws = pl.ds(pl.multiple_of(i * pair, pair), pair)
            yacc_ref[...] += _dot(lhs, cur_ref[rws, :])
            st_ref[...] += _dot(lhs, pm_ref[...])
            return carry
        lax.fori_loop(0, steps, step, 0)

    @pl.when(slot == 0)
    def _():
        pipeline(toep0_ref, toep1_ref)

    @pl.when(slot == 1)
    def _():
        pipeline(toep1_ref, toep0_ref)

    st = st_ref[...]
    a = la_ref[...]
    b = lb_ref[...]
    jrow = lax.broadcasted_iota(jnp.int32, (rows, n2), 0) % chunks_per_seq
    sh = 1
    while sh < chunks_per_seq:
        prev = jnp.where(jrow >= sh, pltpu.roll(st, sh, axis=0), 0.0)
        st = st + prev * a + pltpu.roll(prev, SSM_STATE, axis=1) * b
        a, b = a * a - b * b, 2.0 * a * b
        sh *= 2
    carried = jnp.where(jrow >= 1, pltpu.roll(st, 1, axis=0), 0.0)
    yacc_ref[...] += _dot_nt(carried.astype(BF16), cmt_ref[...])

    for cp in range(gc):
        tmp_ref[cp % 4, pl.ds(cp // 4, rows, stride=4), :] = yacc_ref[:, cp * L:(cp + 1) * L]
    for a4 in range(4):
        yt_ref[pl.ds(a4, quarter, stride=4), :] = tmp_ref[a4]


def _ssm_core(ut, rpt, fpt, ba, bb, ca, cb, lam_a, lam_b, ctile, pw, chunks_per_seq):
    groups, group_rows, L = ut.shape
    gc, n2 = SSM_GROUP_CH, 2 * SSM_STATE
    rows = group_rows // gc
    per_group = lambda g: (g, 0, 0)
    next_group = lambda g: (jnp.minimum(g + 1, groups - 1), 0, 0)
    return pl.pallas_call(
        functools.partial(_ssm_core_kernel, chunks_per_seq=chunks_per_seq),
        out_shape=jax.ShapeDtypeStruct((groups, group_rows, L), F32),
        grid=(groups,),
        in_specs=[
            pl.BlockSpec((None, group_rows, L), per_group),
            pl.BlockSpec((None, L, n2), per_group),
            pl.BlockSpec((None, L, n2), per_group),
            pl.BlockSpec((None, gc, n2), per_group),
            pl.BlockSpec((None, gc, n2), per_group),
            pl.BlockSpec((None, gc, n2), per_group),
            pl.BlockSpec((None, gc, n2), per_group),
            pl.BlockSpec((None, 1, n2), per_group),
            pl.BlockSpec((None, 1, n2), per_group),
            pl.BlockSpec((None, gc, n2), per_group),
            pl.BlockSpec((None, n2, L), per_group),
            pl.BlockSpec((None, gc, n2), next_group),
            pl.BlockSpec((None, gc, n2), next_group),
            pl.BlockSpec((None, gc, n2), next_group),
            pl.BlockSpec((None, n2, L), next_group),
        ],
        out_specs=pl.BlockSpec((None, group_rows, L), per_group),
        scratch_shapes=[pltpu.VMEM((gc * L, gc * L), BF16),
                        pltpu.VMEM((gc * L, gc * L), BF16),
                        pltpu.VMEM((gc * gc, L), F32),
                        pltpu.VMEM((gc * gc, n2), F32),
                        pltpu.VMEM((2 * L, n2), BF16),
                        pltpu.VMEM((gc * L, n2), BF16),
                        pltpu.VMEM((rows, gc * L), F32),
                        pltpu.VMEM((rows, n2), F32),
                        pltpu.VMEM((4, rows * gc // 4, L), F32)],
        compiler_params=_params("arbitrary"),
        name="ssm_chunk_conv",
    )(ut, rpt, fpt, ba, bb, ca, cb, lam_a, lam_b, ctile, pw, ctile, ba, bb, pw)


def _ssm_operators(a_re, a_im, log_dt, b_re, b_im, c_re, c_im):
    L = SSM_CHUNK
    a_re = a_re.astype(F32)
    a_im = a_im.astype(F32)
    dt = jnp.exp(log_dt.astype(F32))[:, None]
    mag = jnp.exp(a_re * dt)
    lam_re = mag * jnp.cos(a_im * dt)
    lam_im = mag * jnp.sin(a_im * dt)
    nr, ni = lam_re - 1.0, lam_im
    den = a_re * a_re + a_im * a_im
    coef_re = ((nr * a_re + ni * a_im) / den)[..., None]
    coef_im = ((ni * a_re - nr * a_im) / den)[..., None]
    b_re = b_re.astype(F32)
    b_im = b_im.astype(F32)
    bb_re = coef_re * b_re - coef_im * b_im
    bb_im = coef_re * b_im + coef_im * b_re
    c_re = c_re.astype(F32)
    c_im = c_im.astype(F32)

    k = jnp.arange(L + 1, dtype=F32)
    pmag = jnp.exp((a_re * dt)[..., None] * k)
    ang = (a_im * dt)[..., None] * k
    pw_re = pmag * jnp.cos(ang)
    pw_im = pmag * jnp.sin(ang)

    def lanes(re, im):
        return jnp.concatenate([jnp.swapaxes(re, 1, 2), jnp.swapaxes(im, 1, 2)], axis=-1)

    rpt = lanes(jnp.flip(pw_re[..., :L], axis=-1), jnp.flip(pw_im[..., :L], axis=-1))
    fpt = lanes(pw_re[..., 1:], pw_im[..., 1:])
    ba = lanes(bb_re, bb_re)
    bb = lanes(-bb_im, bb_im)
    ca = jnp.concatenate([c_re, -c_re], axis=-1)
    cb = jnp.concatenate([-c_im, -c_im], axis=-1)
    lam_a = jnp.concatenate([pw_re[..., L], pw_re[..., L]], axis=-1)[:, None, :]
    lam_b = jnp.concatenate([-pw_im[..., L], pw_im[..., L]], axis=-1)[:, None, :]
    ctile = jnp.concatenate([c_re, c_im], axis=-1)
    pw = jnp.concatenate([pw_re[..., :L], pw_im[..., :L]], axis=1)
    return rpt, fpt, ba, bb, ca, cb, lam_a, lam_b, ctile, pw


def _ssm_out_kernel(yt_ref, ut_ref, d_ref, wg_ref, bg_ref, wo_ref, x_ref, g_ref, b_ref, out_ref, *, sub):
    gc = SSM_GROUP_CH
    groups, nq = yt_ref.shape[0], yt_ref.shape[1] // gc
    qs = sub // SSM_CHUNK
    n_sub = nq // qs

    def channel_major(ref, q0):
        return jnp.concatenate(
            [jnp.concatenate([ref[g, q * gc:(q + 1) * gc, :] for g in range(groups)], axis=0)
             for q in range(q0, q0 + qs)], axis=1)

    def gelu_gate(s):
        y = channel_major(yt_ref, s * qs) + d_ref[...] * channel_major(ut_ref, s * qs)
        z = 0.5 * y * (1.0 + jnp.tanh(math.sqrt(2.0 / math.pi) * (y + 0.044715 * (y * y * y))))
        return z, _dot(wg_ref[...], z.astype(BF16)) + bg_ref[...]

    def glu_out(s, z, gate):
        rows = slice(s * sub, (s + 1) * sub)
        zs = z * (1.0 / (1.0 + jnp.exp(-gate)))
        f = _dot_tn(zs.astype(BF16), wo_ref[...])
        r = DEEPNORM_ALPHA * x_ref[rows, :] + f
        out_ref[rows, :] = _layer_norm(r, g_ref[...], b_ref[...])

    pending = gelu_gate(0)
    for s in range(n_sub):
        upcoming = gelu_gate(s + 1) if s + 1 < n_sub else None
        glu_out(s, *pending)
        pending = upcoming


def _ssm_out(yt, ut, d_skip, w_glu_t_bf16, b_glu, w_out_bf16, x2d, ln_g, ln_b, tn, sub):
    tokens, d_model = x2d.shape
    groups = yt.shape[0]
    d_ssm = groups * SSM_GROUP_CH
    fixed = lambda i: (0, 0)
    chunked = pl.BlockSpec((groups, tn // SSM_CHUNK * SSM_GROUP_CH, SSM_CHUNK), lambda i: (0, i, 0))
    return pl.pallas_call(
        functools.partial(_ssm_out_kernel, sub=sub),
        out_shape=jax.ShapeDtypeStruct((tokens, d_model), F32),
        grid=(tokens // tn,),
        in_specs=[
            chunked,
            chunked,
            pl.BlockSpec((d_ssm, 1), fixed),
            pl.BlockSpec((d_ssm, d_ssm), fixed),
            pl.BlockSpec((d_ssm, 1), fixed),
            pl.BlockSpec((d_ssm, d_model), fixed),
            pl.BlockSpec((tn, d_model), lambda i: (i, 0)),
            pl.BlockSpec((1, d_model), fixed),
            pl.BlockSpec((1, d_model), fixed),
        ],
        out_specs=pl.BlockSpec((tn, d_model), lambda i: (i, 0)),
        compiler_params=_params("parallel"),
        name="ssm_glu_out_ln",
    )(yt, ut, d_skip.reshape(-1, 1), w_glu_t_bf16, b_glu.reshape(-1, 1), w_out_bf16, x2d,
      ln_g.reshape(1, -1), ln_b.reshape(1, -1))


def _rope_tables(seq):
    inv_freq = ROPE_THETA ** (-jnp.arange(0, HEAD_DIM, 2, dtype=F32) / HEAD_DIM)
    inv_freq = jnp.concatenate([inv_freq, inv_freq])[None, :]
    ang_a = (jnp.arange(seq // ROPE_BLOCK, dtype=F32) * ROPE_BLOCK)[:, None] * inv_freq
    ang_b = jnp.arange(ROPE_BLOCK, dtype=F32)[:, None] * inv_freq
    sign = jnp.where(jnp.arange(HEAD_DIM) < HEAD_DIM // 2, -1.0, 1.0).astype(F32)[None, :]
    cb, sb = jnp.cos(ang_b), jnp.sin(ang_b)
    return jnp.cos(ang_a), jnp.sin(ang_a), cb, sb, cb * sign, sb * sign


def _row_tile(seq, want):
    tm = min(want, seq)
    assert seq % tm == 0
    return tm


def kernel(x, attn_w_in, attn_w_out, ssm_w_in, ssm_a_re, ssm_a_im, ssm_log_dt, ssm_b_re, ssm_b_im,
           ssm_c_re, ssm_c_im, ssm_d, ssm_w_glu, ssm_b_glu, ssm_w_out, ffn_w_up, ffn_conv_w,
           ffn_conv_b, ffn_w_down, ln_g, ln_b):
    batch, seq, d_model = x.shape
    assert seq % DILATION_PAIRS[-1][0] == 0 and seq % SSM_CHUNK == 0
    assert all(w // d == ATTN_BLOCK for w, d in DILATION_PAIRS)
    h = x.reshape(batch * seq, d_model)
    ffn_chunk = 256

    w_up_all = ffn_w_up.astype(BF16)
    w_down_all = ffn_w_down.astype(BF16)

    def ffn(h, i, with_bf16_copy=False):
        return _conv_ffn(h, w_up_all, ffn_conv_w[i], ffn_conv_b[i], w_down_all, ln_g[i, 1], ln_b[i, 1],
                         i, seq, _row_tile(seq, 1024), ffn_chunk, with_bf16_copy)

    rope = _rope_tables(seq)
    w_in = attn_w_in[0].astype(BF16)
    os, lses = [], []
    for g, (_, dilation) in enumerate(DILATION_PAIRS):
        qkv_g = _qkv_rope(h, w_in, rope, g, dilation, batch, seq, _row_tile(seq, 1024))
        o, lse = _dilated_attention_group(qkv_g, g, 2048)
        os.append(o)
        lses.append(lse)
    h = _merge_out(os, lses, h, attn_w_out[0].astype(BF16), ln_g[0, 0], ln_b[0, 0], seq,
                   _row_tile(seq, 512))
    h, h_bf16 = ffn(h, 0, with_bf16_copy=True)

    ssm_ops = _ssm_operators(
        ssm_a_re[0], ssm_a_im[0], ssm_log_dt[0], ssm_b_re[0], ssm_b_im[0], ssm_c_re[0], ssm_c_im[0])
    ut = _ssm_in(h_bf16, ssm_w_in[0].T.astype(BF16), _row_tile(seq, 1024))
    yt = _ssm_core(ut, *ssm_ops, seq // SSM_CHUNK)
    h = _ssm_out(yt, ut, ssm_d[0], ssm_w_glu[0].T.astype(BF16), ssm_b_glu[0],
                 ssm_w_out[0].astype(BF16), h, ln_g[1, 0], ln_b[1, 0], _row_tile(seq, 1024), 512)
    h = ffn(h, 1)
    return h.reshape(batch, seq, d_model)
```

```python
import functools
import math

import jax
import jax.numpy as jnp
from jax import lax
from jax.experimental import pallas as pl
from jax.experimental.pallas import tpu as pltpu

F32 = jnp.float32
BF16 = jnp.bfloat16

DEPTH = 2
DILATION_PAIRS = ((128, 1), (512, 4), (2048, 16))
N_GROUPS = len(DILATION_PAIRS)
HEADS = 8
HEAD_DIM = 128
ROPE_THETA = 10000.0
SSM_GROUP_CH = 16
SSM_STATE = 64
CONV_WIDTH = 3
DEEPNORM_ALPHA = (2.0 * DEPTH) ** 0.25
LN_EPS = 1e-5
NEG_INF = -1e30
LOG2E = math.log2(math.e)
LN2 = math.log(2.0)

LANES = 128
SUBLANES = 8
VMEM_LIMIT_BYTES = 56 * 1024 * 1024

ATTN_BLOCK = 128
ROPE_BLOCK = 128
PERM_ROWS = 256
SSM_CHUNK = 128


def _params(*semantics):
    return pltpu.CompilerParams(dimension_semantics=semantics,
                                vmem_limit_bytes=VMEM_LIMIT_BYTES)


def _layer_norm(r, g, b):
    mu = jnp.mean(r, axis=-1, keepdims=True)
    d = r - mu
    var = jnp.mean(d * d, axis=-1, keepdims=True)
    return d * lax.rsqrt(var + LN_EPS) * g + b


def _dot(a, b):
    return jnp.dot(a, b, preferred_element_type=F32)


def _dot_nt(a, b):
    return lax.dot_general(a, b, (((1,), (1,)), ((), ())), preferred_element_type=F32)


def _dot_tn(a, b):
    return lax.dot_general(a, b, (((0,), (0,)), ((), ())), preferred_element_type=F32)


def _residue_rows(r, n, d):
    return pl.ds(r, n, stride=d) if d > 1 else slice(None)


def _qkv_kernel(*refs, d):
    n_slabs = len(refs) - 14
    x_refs = refs[:n_slabs]
    (w_ref, ca_ref, sa_ref, cb_ref, sb_ref, cbs_ref, sbs_ref, o_ref,
     xb_ref, cos_ref, sin_ref, cs_ref, sn_ref, tmp_ref) = refs[n_slabs:]
    tm = cos_ref.shape[0]
    n = tm // d
    hw = HEADS * HEAD_DIM

    for q in range(tm // ROPE_BLOCK):
        blk = slice(q * ROPE_BLOCK, (q + 1) * ROPE_BLOCK)
        ca = ca_ref[q:q + 1, :]
        sa = sa_ref[q:q + 1, :]
        cos_ref[blk, :] = ca * cb_ref[...] - sa * sb_ref[...]
        sin_ref[blk, :] = sa * cbs_ref[...] + ca * sbs_ref[...]

    def residues(src_ref):
        if d % 8:
            return [src_ref[_residue_rows(r, n, d), :] for r in range(d)]
        for a in range(4):
            tmp_ref[a] = src_ref[pl.ds(a, tm // 4, stride=4), :]
        return [tmp_ref[r % 4, pl.ds(r // 4, n, stride=d // 4), :] for r in range(d)]

    for c, x_ref in enumerate(x_refs):
        for r, piece in enumerate(residues(x_ref)):
            xb_ref[r * n:(r + 1) * n, c * LANES:(c + 1) * LANES] = piece.astype(BF16)
    for src_ref, dst_ref in ((cos_ref, cs_ref), (sin_ref, sn_ref)):
        for r, piece in enumerate(residues(src_ref)):
            dst_ref[r * n:(r + 1) * n, :] = piece
    xb = xb_ref[...]

    for kind in range(3):
        acc = _dot(xb, w_ref[:, kind * hw:(kind + 1) * hw])
        if kind == 2:
            for r in range(d):
                o_ref[r, :, kind * hw:(kind + 1) * hw] = acc[r * n:(r + 1) * n, :].astype(BF16)
            continue
        scale = HEAD_DIM ** -0.5 * LOG2E if kind == 0 else 1.0
        cos = cs_ref[...] * scale
        sin = sn_ref[...] * scale
        for h in range(HEADS):
            t = acc[:, h * HEAD_DIM:(h + 1) * HEAD_DIM]
            rot = (t * cos + pltpu.roll(t, HEAD_DIM // 2, axis=1) * sin).astype(BF16)
            lanes = slice(kind * hw + h * HEAD_DIM, kind * hw + (h + 1) * HEAD_DIM)
            for r in range(d):
                o_ref[r, :, lanes] = rot[r * n:(r + 1) * n, :]


def _qkv_rope(x2d, w_in_bf16, rope, g, d, batch, seq, tm):
    d_model = x2d.shape[1]
    hw = HEADS * HEAD_DIM
    seq_tiles = seq // tm
    n = tm // d
    coarse = pl.BlockSpec((tm // ROPE_BLOCK, HEAD_DIM), lambda i: (i % seq_tiles, 0))
    fine = pl.BlockSpec((ROPE_BLOCK, HEAD_DIM), lambda i: (0, 0))
    return pl.pallas_call(
        functools.partial(_qkv_kernel, d=d),
        out_shape=jax.ShapeDtypeStruct((batch, d, seq // d, 3 * hw), BF16),
        grid=(batch * seq_tiles,),
        in_specs=[pl.BlockSpec((tm, LANES), lambda i, c=c: (i, c))
                  for c in range(d_model // LANES)] + [
            pl.BlockSpec((d_model, 3 * hw), lambda i: (0, g)),
            coarse, coarse, fine, fine, fine, fine,
        ],
        out_specs=pl.BlockSpec((None, d, n, 3 * hw),
                               lambda i: (i // seq_tiles, 0, i % seq_tiles, 0)),
        scratch_shapes=[pltpu.VMEM((tm, d_model), BF16),
                        pltpu.VMEM((tm, HEAD_DIM), F32),
                        pltpu.VMEM((tm, HEAD_DIM), F32),
                        pltpu.VMEM((tm, HEAD_DIM), F32),
                        pltpu.VMEM((tm, HEAD_DIM), F32),
                        pltpu.VMEM((4, tm // 4, LANES), F32)],
        compiler_params=_params("parallel"),
        name=f"qkv_rope_g{g}",
    )(*([x2d] * (d_model // LANES)), w_in_bf16, *rope)


def _attn_kernel(q_ref, k_ref, v_ref, kp_ref, vp_ref, o_ref, lse_ref, *, nblk):
    m = pl.program_id(2)
    blk_rows = ATTN_BLOCK
    row = lax.broadcasted_iota(jnp.int32, (blk_rows, 2 * blk_rows), 0)
    col = lax.broadcasted_iota(jnp.int32, (blk_rows, 2 * blk_rows), 1)
    band = jnp.logical_and(col >= row, col <= row + blk_rows)
    first_band = jnp.logical_and(band, jnp.logical_or(col >= blk_rows, m > 0))
    lane = lax.broadcasted_iota(jnp.int32, (blk_rows, LANES), 1)
    ones = jnp.ones((2 * blk_rows, HEAD_DIM), BF16)

    def block(ri, q_rows, keys, vals, mask):
        mx_tile = jnp.zeros((blk_rows, LANES), F32)
        l_tile = jnp.ones((blk_rows, LANES), F32)
        for h in range(HEADS):
            cols = slice(h * HEAD_DIM, (h + 1) * HEAD_DIM)
            s = jnp.where(mask, _dot_nt(q_ref[ri, q_rows, cols], keys(cols)), NEG_INF)
            mx = jnp.max(jnp.maximum(s[:, :blk_rows], s[:, blk_rows:]), axis=1, keepdims=True)
            p = jnp.exp2(s - mx).astype(BF16)
            oe = _dot(p, jnp.concatenate([vals(cols), ones], axis=1))
            l = oe[:, HEAD_DIM:]
            o_ref[ri, q_rows, cols] = (oe[:, :HEAD_DIM] / l).astype(BF16)
            mx_tile = jnp.where(lane == h, mx, mx_tile)
            l_tile = jnp.where(lane == h, l, l_tile)
        lse_ref[ri, q_rows, :] = mx_tile * LN2 + jnp.log(l_tile)

    head = slice(0, blk_rows)
    for ri in range(q_ref.shape[0]):
        block(ri, head,
              lambda cols: jnp.concatenate([kp_ref[ri, :, cols], k_ref[ri, head, cols]], axis=0),
              lambda cols: jnp.concatenate([vp_ref[ri, :, cols], v_ref[ri, head, cols]], axis=0),
              first_band)
        for blk in range(1, nblk):
            q_rows = slice(blk * blk_rows, (blk + 1) * blk_rows)
            kv_rows = slice((blk - 1) * blk_rows, (blk + 1) * blk_rows)
            block(ri, q_rows, lambda cols: k_ref[ri, kv_rows, cols],
                  lambda cols: v_ref[ri, kv_rows, cols], band)


def _dilated_attention_group(qkv_g, g, rows):
    batch, d, sub_len, _ = qkv_g.shape
    hw = HEADS * HEAD_DIM
    res = max(1, min(d, rows // sub_len))
    rows = min(rows, sub_len)
    nblk = rows // ATTN_BLOCK

    def main_spec(kind):
        return pl.BlockSpec((None, res, rows, hw), lambda b, r, m: (b, r, m, kind))

    def prev_spec(kind):
        return pl.BlockSpec((None, res, ATTN_BLOCK, hw),
                            lambda b, r, m: (b, r, jnp.maximum(m * nblk - 1, 0), kind))

    return pl.pallas_call(
        functools.partial(_attn_kernel, nblk=nblk),
        out_shape=(jax.ShapeDtypeStruct((batch, d, sub_len, hw), BF16),
                   jax.ShapeDtypeStruct((batch, d, sub_len, LANES), F32)),
        grid=(batch, d // res, sub_len // rows),
        in_specs=[main_spec(0), main_spec(1), main_spec(2), prev_spec(1), prev_spec(2)],
        out_specs=(pl.BlockSpec((None, res, rows, hw), lambda b, r, m: (b, r, m, 0)),
                   pl.BlockSpec((None, res, rows, LANES), lambda b, r, m: (b, r, m, 0))),
        compiler_params=_params("parallel", "parallel", "arbitrary"),
        name=f"dilated_attn_g{g}",
    )(qkv_g, qkv_g, qkv_g, qkv_g, qkv_g)


def _merge_out_kernel(o0_ref, o1_ref, o2_ref, l0_ref, l1_ref, l2_ref, e_ref, p1_ref, p2_ref, x_ref, w_ref,
                      g_ref, b_ref, out_ref, ln_ref):
    tm = x_ref.shape[0]
    groups = ((o0_ref, l0_ref, None), (o1_ref, l1_ref, p1_ref), (o2_ref, l2_ref, p2_ref))
    for gi, (o_ref, l_ref, _) in enumerate(groups):
        d = o_ref.shape[0]
        n = tm // d
        if d == 1:
            continue
        for r in range(d):
            ln_ref[gi, _residue_rows(r, n, d), :] = l_ref[r]
    e = e_ref[...]
    for h in range(tm // PERM_ROWS):
        rws = slice(h * PERM_ROWS, (h + 1) * PERM_ROWS)
        outs, lses = [], []
        for gi, (o_ref, l_ref, p_ref) in enumerate(groups):
            d = o_ref.shape[0]
            if d == 1:
                outs.append(o_ref[0, rws, :].astype(F32))
                lses.append(l_ref[0, rws, :])
                continue
            m = PERM_ROWS // d
            stacked = jnp.concatenate([o_ref[r, h * m:(h + 1) * m, :] for r in range(d)], axis=0)
            outs.append(_dot(p_ref[...], stacked))
            lses.append(ln_ref[gi, rws, :])
        mx = jnp.maximum(jnp.maximum(lses[0], lses[1]), lses[2])
        ws = [jnp.exp(l - mx) for l in lses]
        inv = 1.0 / (ws[0] + ws[1] + ws[2])
        merged = None
        for w, o in zip(ws, outs):
            w = w * inv
            w_hi = w.astype(BF16)
            w_lo = (w - w_hi.astype(F32)).astype(BF16)
            w_full = _dot(jnp.concatenate([w_hi, w_lo], axis=1), e)
            term = w_full * o
            merged = term if merged is None else merged + term
        y = _dot(merged.astype(BF16), w_ref[...])
        r = DEEPNORM_ALPHA * x_ref[rws, :] + y
        out_ref[rws, :] = _layer_norm(r, g_ref[...], b_ref[...])


def _token_order_matrix(d):
    t = jnp.arange(PERM_ROWS)
    p = (t % d) * (PERM_ROWS // d) + t // d
    return (p[:, None] == jnp.arange(PERM_ROWS)[None, :]).astype(BF16)


def _merge_out(os, lses, x2d, w_out_bf16, ln_g, ln_b, seq, tm):
    tokens, d_model = x2d.shape
    hw = HEADS * HEAD_DIM
    seq_tiles = seq // tm
    expand = (jnp.arange(2 * LANES)[:, None] % LANES == (jnp.arange(hw)[None, :] // HEAD_DIM)).astype(BF16)
    row = lambda i: (i, 0)
    fixed = lambda i: (0, 0)

    def group_spec(arr):
        d, width = arr.shape[1], arr.shape[3]
        return pl.BlockSpec((None, d, tm // d, width),
                            lambda i: (i // seq_tiles, 0, i % seq_tiles, 0))

    perms = [_token_order_matrix(a.shape[1]) for a in os[1:]]
    return pl.pallas_call(
        _merge_out_kernel,
        out_shape=jax.ShapeDtypeStruct((tokens, d_model), F32),
        grid=(tokens // tm,),
        in_specs=[group_spec(a) for a in os] + [group_spec(a) for a in lses] + [
            pl.BlockSpec((2 * LANES, hw), fixed),
            pl.BlockSpec((PERM_ROWS, PERM_ROWS), fixed),
            pl.BlockSpec((PERM_ROWS, PERM_ROWS), fixed),
            pl.BlockSpec((tm, d_model), row),
            pl.BlockSpec((hw, d_model), fixed),
            pl.BlockSpec((1, d_model), fixed),
            pl.BlockSpec((1, d_model), fixed),
        ],
        out_specs=pl.BlockSpec((tm, d_model), row),
        scratch_shapes=[pltpu.VMEM((N_GROUPS, tm, LANES), F32)],
        compiler_params=_params("parallel"),
        name="attn_merge_out_ln",
    )(*os, *lses, expand, *perms, x2d, w_out_bf16, ln_g.reshape(1, -1), ln_b.reshape(1, -1))


def _ffn_kernel(x_ref, halo_ref, wup_ref, cw_ref, cb_ref, wdown_ref, g_ref, b_ref, out_ref, *rest,
                seq_tiles, d_ff, chunk, ln_split):
    outb_ref = rest[0] if len(rest) == 3 else None
    xe_ref, h_ref = rest[-2:]
    i = pl.program_id(0)
    x = x_ref[...]
    first = (i % seq_tiles) == 0
    xe_ref[0:SUBLANES, :] = jnp.where(first, 0.0, halo_ref[...]).astype(BF16)
    xe_ref[SUBLANES:, :] = x.astype(BF16)
    xe = xe_ref[...]
    for c in range(d_ff // chunk):
        gcols = slice(c * chunk, (c + 1) * chunk)
        vcols = slice(d_ff + c * chunk, d_ff + (c + 1) * chunk)
        gate = _dot(xe, wup_ref[:, gcols])
        val = _dot(xe[SUBLANES:, :], wup_ref[:, vcols])
        conv = (cb_ref[:, gcols]
                + cw_ref[0:1, gcols] * pltpu.roll(gate, 2, axis=0)[SUBLANES:, :]
                + cw_ref[1:2, gcols] * pltpu.roll(gate, 1, axis=0)[SUBLANES:, :]
                + cw_ref[2:3, gcols] * gate[SUBLANES:, :])
        act = conv * (1.0 / (1.0 + jnp.exp(-conv)))
        h_ref[:, gcols] = (act * val).astype(BF16)
    part = x_ref.shape[0] // ln_split
    for s in range(ln_split):
        rows = slice(s * part, (s + 1) * part)
        f = _dot(h_ref[rows, :], wdown_ref[...])
        r = DEEPNORM_ALPHA * x_ref[rows, :] + f
        y = _layer_norm(r, g_ref[...], b_ref[...])
        out_ref[rows, :] = y
        if outb_ref is not None:
            outb_ref[rows, :] = y.astype(BF16)


def _conv_ffn(x2d, w_up_bf16, conv_w, conv_b, w_down_bf16, ln_g, ln_b, layer, seq, tm, chunk,
              with_bf16_copy=False):
    tokens, d_model = x2d.shape
    d_ff = w_down_bf16.shape[1]
    seq_tiles = seq // tm
    halo_blocks = tm // SUBLANES
    fixed = lambda i: (0, 0)
    of_layer = lambda i: (layer, 0, 0)
    single = dict(pipeline_mode=pl.Buffered(1))
    in_specs = [
        pl.BlockSpec((tm, d_model), lambda i: (i, 0)),
        pl.BlockSpec((SUBLANES, d_model), lambda i: (jnp.maximum(i * halo_blocks - 1, 0), 0)),
        pl.BlockSpec((None, d_model, 2 * d_ff), of_layer, **single),
        pl.BlockSpec((CONV_WIDTH, d_ff), fixed),
        pl.BlockSpec((1, d_ff), fixed),
        pl.BlockSpec((None, d_ff, d_model), of_layer, **single),
        pl.BlockSpec((1, d_model), fixed),
        pl.BlockSpec((1, d_model), fixed),
    ]
    operands = [x2d, x2d, w_up_bf16, conv_w, conv_b.reshape(1, -1), w_down_bf16,
                ln_g.reshape(1, -1), ln_b.reshape(1, -1)]
    out_shape = jax.ShapeDtypeStruct((tokens, d_model), F32)
    out_specs = pl.BlockSpec((tm, d_model), lambda i: (i, 0))
    if with_bf16_copy:
        out_shape = (out_shape, jax.ShapeDtypeStruct((tokens, d_model), BF16))
        out_specs = (out_specs, pl.BlockSpec((tm, d_model), lambda i: (i, 0)))
    return pl.pallas_call(
        functools.partial(_ffn_kernel, seq_tiles=seq_tiles, d_ff=d_ff, chunk=chunk, ln_split=4),
        out_shape=out_shape,
        grid=(tokens // tm,),
        in_specs=in_specs,
        out_specs=out_specs,
        scratch_shapes=[pltpu.VMEM((tm + SUBLANES, d_model), BF16),
                        pltpu.VMEM((tm, d_ff), BF16)],
        compiler_params=_params("parallel"),
        name="conv_ffn_ln",
    )(*operands)


def _ssm_in_kernel(x_ref, wt_ref, ut_ref):
    gc = SSM_GROUP_CH
    u = _dot_nt(wt_ref[...], x_ref[...])
    for g in range(ut_ref.shape[0]):
        for q in range(ut_ref.shape[1] // gc):
            ut_ref[g, q * gc:(q + 1) * gc, :] = u[g * gc:(g + 1) * gc, q * SSM_CHUNK:(q + 1) * SSM_CHUNK]


def _ssm_in(x2d_bf16, w_in_t_bf16, tn):
    tokens, d_model = x2d_bf16.shape
    d_ssm = w_in_t_bf16.shape[0]
    gc = SSM_GROUP_CH
    groups = d_ssm // gc
    return pl.pallas_call(
        _ssm_in_kernel,
        out_shape=jax.ShapeDtypeStruct((groups, tokens // SSM_CHUNK * gc, SSM_CHUNK), F32),
        grid=(tokens // tn,),
        in_specs=[pl.BlockSpec((tn, d_model), lambda i: (i, 0)),
                  pl.BlockSpec((d_ssm, d_model), lambda i: (0, 0))],
        out_specs=pl.BlockSpec((groups, tn // SSM_CHUNK * gc, SSM_CHUNK), lambda i: (0, i, 0)),
        compiler_params=_params("parallel"),
        name="ssm_in_proj",
    )(x2d_bf16, w_in_t_bf16)


def _split_bf16(x):
    hi = x.astype(BF16)
    return hi, (x - hi.astype(F32)).astype(BF16)


def _ssm_core_kernel(ut_ref, rpt_ref, fpt_ref, ba_ref, bb_ref, ca_ref, cb_ref, la_ref, lb_ref,
                     ct_ref, pw_ref, ctn_ref, ban_ref, bbn_ref, pwn_ref,
                     yt_ref, toep0_ref, toep1_ref, kv_ref, cbs_ref, pm_ref, cmt_ref, yacc_ref, st_ref,
                     tmp_ref, *, chunks_per_seq):
    gc = SSM_GROUP_CH
    L = SSM_CHUNK
    rows = ut_ref.shape[0] // gc
    n2 = 2 * SSM_STATE
    pair = 2 * L
    steps = gc // 2
    g = pl.program_id(0)
    slot = g % 2

    srow = lax.broadcasted_iota(jnp.int32, (L, L), 0)
    tcol = lax.broadcasted_iota(jnp.int32, (L, L), 1)
    causal = tcol >= srow
    conj = jnp.where(lax.broadcasted_iota(jnp.int32, (1, n2), 1) < SSM_STATE, 1.0, -1.0)

    def taps(c_ref, a_ref, b_ref, p_ref):
        ct = c_ref[...]
        ct_sw = pltpu.roll(ct, SSM_STATE, axis=1)
        for c in range(gc):
            cb_c = ct * a_ref[c:c + 1, :] + ct_sw * b_ref[c:c + 1, :]
            cbs_ref[c * gc:(c + 1) * gc, :] = cb_c * conj
        m_hi, m_lo = _split_bf16(cbs_ref[...])
        p_hi, p_lo = _split_bf16(p_ref[...])
        kv_ref[...] = _dot(m_hi, p_hi) + (_dot(m_hi, p_lo) + _dot(m_lo, p_hi))

    def build_pair(i, dst_ref):
        for cc in range(2):
            c = 2 * i + cc
            for cp in range(gc):
                kv = kv_ref[pl.ds(c * gc + cp, 1), :]
                blk = pltpu.roll(jnp.broadcast_to(kv, (L, L)), 0, 1, stride=1, stride_axis=0)
                blk = jnp.where(causal, blk, 0.0)
                dst_ref[pl.ds(pl.multiple_of(c * L, L), L), cp * L:(cp + 1) * L] = blk.astype(BF16)

    @pl.when(g == 0)
    def _():
        taps(ct_ref, ba_ref, bb_ref, pw_ref)

        def first(i, carry):
            build_pair(i, toep0_ref)
            return carry
        lax.fori_loop(0, steps, first, 0)

    taps(ctn_ref, ban_ref, bbn_ref, pwn_ref)

    quarter = rows * gc // 4
    for a in range(4):
        tmp_ref[a] = ut_ref[pl.ds(a, quarter, stride=4), :]

    rpt = rpt_ref[...]
    rpt_sw = pltpu.roll(rpt, SSM_STATE, axis=1)
    fpt = fpt_ref[...]
    fpt_sw = pltpu.roll(fpt, SSM_STATE, axis=1)
    st_ref[...] = jnp.zeros_like(st_ref)
    yacc_ref[...] = jnp.zeros_like(yacc_ref)

    def pipeline(cur_ref, nxt_ref):
        def step(i, carry):
            build_pair(i, nxt_ref)
            halves = []
            for cc in range(2):
                c = 2 * i + cc
                halves.append(tmp_ref[c % 4, pl.ds(c // 4, rows, stride=4), :].astype(BF16))
                crow = pl.ds(c, 1)
                pm_ref[cc * L:(cc + 1) * L, :] = (rpt * ba_ref[crow, :] + rpt_sw * bb_ref[crow, :]).astype(BF16)
                cmt_ref[pl.ds(pl.multiple_of(c * L, L), L), :] = (
                    fpt * ca_ref[crow, :] + fpt_sw * cb_ref[crow, :]).astype(BF16)
            lhs = jnp.concatenate(halves, axis=1)
            rws = pl.ds(pl.multiple_of(i * pair, pair), pair)
            yacc_ref[...] += _dot(lhs, cur_ref[rws, :])
            st_ref[...] += _dot(lhs, pm_ref[...])
            return carry
        lax.fori_loop(0, steps, step, 0)

    @pl.when(slot == 0)
    def _():
        pipeline(toep0_ref, toep1_ref)

    @pl.when(slot == 1)
    def _():
        pipeline(toep1_ref, toep0_ref)

    st = st_ref[...]
    a = la_ref[...]
    b = lb_ref[...]
    jrow = lax.broadcasted_iota(jnp.int32, (rows, n2), 0) % chunks_per_seq
    sh = 1
    while sh < chunks_per_seq:
        prev = jnp.where(jrow >= sh, pltpu.roll(st, sh, axis=0), 0.0)
        st = st + prev * a + pltpu.roll(prev, SSM_STATE, axis=1) * b
        a, b = a * a - b * b, 2.0 * a * b
        sh *= 2
    carried = jnp.where(jrow >= 1, pltpu.roll(st, 1, axis=0), 0.0)
    yacc_ref[...] += _dot_nt(carried.astype(BF16), cmt_ref[...])

    for cp in range(gc):
        tmp_ref[cp % 4, pl.ds(cp // 4, rows, stride=4), :] = yacc_ref[:, cp * L:(cp + 1) * L]
    for a4 in range(4):
        yt_ref[pl.ds(a4, quarter, stride=4), :] = tmp_ref[a4]


def _ssm_core(ut, rpt, fpt, ba, bb, ca, cb, lam_a, lam_b, ctile, pw, chunks_per_seq):
    groups, group_rows, L = ut.shape
    gc, n2 = SSM_GROUP_CH, 2 * SSM_STATE
    rows = group_rows // gc
    per_group = lambda g: (g, 0, 0)
    next_group = lambda g: (jnp.minimum(g + 1, groups - 1), 0, 0)
    return pl.pallas_call(
        functools.partial(_ssm_core_kernel, chunks_per_seq=chunks_per_seq),
        out_shape=jax.ShapeDtypeStruct((groups, group_rows, L), F32),
        grid=(groups,),
        in_specs=[
            pl.BlockSpec((None, group_rows, L), per_group),
            pl.BlockSpec((None, L, n2), per_group),
            pl.BlockSpec((None, L, n2), per_group),
            pl.BlockSpec((None, gc, n2), per_group),
            pl.BlockSpec((None, gc, n2), per_group),
            pl.BlockSpec((None, gc, n2), per_group),
            pl.BlockSpec((None, gc, n2), per_group),
            pl.BlockSpec((None, 1, n2), per_group),
            pl.BlockSpec((None, 1, n2), per_group),
            pl.BlockSpec((None, gc, n2), per_group),
            pl.BlockSpec((None, n2, L), per_group),
            pl.BlockSpec((None, gc, n2), next_group),
            pl.BlockSpec((None, gc, n2), next_group),
            pl.BlockSpec((None, gc, n2), next_group),
            pl.BlockSpec((None, n2, L), next_group),
        ],
        out_specs=pl.BlockSpec((None, group_rows, L), per_group),
        scratch_shapes=[pltpu.VMEM((gc * L, gc * L), BF16),
                        pltpu.VMEM((gc * L, gc * L), BF16),
                        pltpu.VMEM((gc * gc, L), F32),
                        pltpu.VMEM((gc * gc, n2), F32),
                        pltpu.VMEM((2 * L, n2), BF16),
                        pltpu.VMEM((gc * L, n2), BF16),
                        pltpu.VMEM((rows, gc * L), F32),
                        pltpu.VMEM((rows, n2), F32),
                        pltpu.VMEM((4, rows * gc // 4, L), F32)],
        compiler_params=_params("arbitrary"),
        name="ssm_chunk_conv",
    )(ut, rpt, fpt, ba, bb, ca, cb, lam_a, lam_b, ctile, pw, ctile, ba, bb, pw)


def _ssm_operators(a_re, a_im, log_dt, b_re, b_im, c_re, c_im):
    L = SSM_CHUNK
    a_re = a_re.astype(F32)
    a_im = a_im.astype(F32)
    dt = jnp.exp(log_dt.astype(F32))[:, None]
    mag = jnp.exp(a_re * dt)
    lam_re = mag * jnp.cos(a_im * dt)
    lam_im = mag * jnp.sin(a_im * dt)
    nr, ni = lam_re - 1.0, lam_im
    den = a_re * a_re + a_im * a_im
    coef_re = ((nr * a_re + ni * a_im) / den)[..., None]
    coef_im = ((ni * a_re - nr * a_im) / den)[..., None]
    b_re = b_re.astype(F32)
    b_im = b_im.astype(F32)
    bb_re = coef_re * b_re - coef_im * b_im
    bb_im = coef_re * b_im + coef_im * b_re
    c_re = c_re.astype(F32)
    c_im = c_im.astype(F32)

    k = jnp.arange(L + 1, dtype=F32)
    pmag = jnp.exp((a_re * dt)[..., None] * k)
    ang = (a_im * dt)[..., None] * k
    pw_re = pmag * jnp.cos(ang)
    pw_im = pmag * jnp.sin(ang)

    def lanes(re, im):
        return jnp.concatenate([jnp.swapaxes(re, 1, 2), jnp.swapaxes(im, 1, 2)], axis=-1)

    rpt = lanes(jnp.flip(pw_re[..., :L], axis=-1), jnp.flip(pw_im[..., :L], axis=-1))
    fpt = lanes(pw_re[..., 1:], pw_im[..., 1:])
    ba = lanes(bb_re, bb_re)
    bb = lanes(-bb_im, bb_im)
    ca = jnp.concatenate([c_re, -c_re], axis=-1)
    cb = jnp.concatenate([-c_im, -c_im], axis=-1)
    lam_a = jnp.concatenate([pw_re[..., L], pw_re[..., L]], axis=-1)[:, None, :]
    lam_b = jnp.concatenate([-pw_im[..., L], pw_im[..., L]], axis=-1)[:, None, :]
    ctile = jnp.concatenate([c_re, c_im], axis=-1)
    pw = jnp.concatenate([pw_re[..., :L], pw_im[..., :L]], axis=1)
    return rpt, fpt, ba, bb, ca, cb, lam_a, lam_b, ctile, pw


def _ssm_out_kernel(yt_ref, ut_ref, d_ref, wg_ref, bg_ref, wo_ref, x_ref, g_ref, b_ref, out_ref, *, sub):
    gc = SSM_GROUP_CH
    groups, nq = yt_ref.shape[0], yt_ref.shape[1] // gc
    qs = sub // SSM_CHUNK
    n_sub = nq // qs

    def channel_major(ref, q0):
        return jnp.concatenate(
            [jnp.concatenate([ref[g, q * gc:(q + 1) * gc, :] for g in range(groups)], axis=0)
             for q in range(q0, q0 + qs)], axis=1)

    def gelu_gate(s):
        y = channel_major(yt_ref, s * qs) + d_ref[...] * channel_major(ut_ref, s * qs)
        z = 0.5 * y * (1.0 + jnp.tanh(math.sqrt(2.0 / math.pi) * (y + 0.044715 * (y * y * y))))
        return z, _dot(wg_ref[...], z.astype(BF16)) + bg_ref[...]

    def glu_out(s, z, gate):
        rows = slice(s * sub, (s + 1) * sub)
        zs = z * (1.0 / (1.0 + jnp.exp(-gate)))
        f = _dot_tn(zs.astype(BF16), wo_ref[...])
        r = DEEPNORM_ALPHA * x_ref[rows, :] + f
        out_ref[rows, :] = _layer_norm(r, g_ref[...], b_ref[...])

    pending = gelu_gate(0)
    for s in range(n_sub):
        upcoming = gelu_gate(s + 1) if s + 1 < n_sub else None
        glu_out(s, *pending)
        pending = upcoming


def _ssm_out(yt, ut, d_skip, w_glu_t_bf16, b_glu, w_out_bf16, x2d, ln_g, ln_b, tn, sub):
    tokens, d_model = x2d.shape
    groups = yt.shape[0]
    d_ssm = groups * SSM_GROUP_CH
    fixed = lambda i: (0, 0)
    chunked = pl.BlockSpec((groups, tn // SSM_CHUNK * SSM_GROUP_CH, SSM_CHUNK), lambda i: (0, i, 0))
    return pl.pallas_call(
        functools.partial(_ssm_out_kernel, sub=sub),
        out_shape=jax.ShapeDtypeStruct((tokens, d_model), F32),
        grid=(tokens // tn,),
        in_specs=[
            chunked,
            chunked,
            pl.BlockSpec((d_ssm, 1), fixed),
            pl.BlockSpec((d_ssm, d_ssm), fixed),
            pl.BlockSpec((d_ssm, 1), fixed),
            pl.BlockSpec((d_ssm, d_model), fixed),
            pl.BlockSpec((tn, d_model), lambda i: (i, 0)),
            pl.BlockSpec((1, d_model), fixed),
            pl.BlockSpec((1, d_model), fixed),
        ],
        out_specs=pl.BlockSpec((tn, d_model), lambda i: (i, 0)),
        compiler_params=_params("parallel"),
        name="ssm_glu_out_ln",
    )(yt, ut, d_skip.reshape(-1, 1), w_glu_t_bf16, b_glu.reshape(-1, 1), w_out_bf16, x2d,
      ln_g.reshape(1, -1), ln_b.reshape(1, -1))


def _rope_tables(seq):
    inv_freq = ROPE_THETA ** (-jnp.arange(0, HEAD_DIM, 2, dtype=F32) / HEAD_DIM)
    inv_freq = jnp.concatenate([inv_freq, inv_freq])[None, :]
    ang_a = (jnp.arange(seq // ROPE_BLOCK, dtype=F32) * ROPE_BLOCK)[:, None] * inv_freq
    ang_b = jnp.arange(ROPE_BLOCK, dtype=F32)[:, None] * inv_freq
    sign = jnp.where(jnp.arange(HEAD_DIM) < HEAD_DIM // 2, -1.0, 1.0).astype(F32)[None, :]
    cb, sb = jnp.cos(ang_b), jnp.sin(ang_b)
    return jnp.cos(ang_a), jnp.sin(ang_a), cb, sb, cb * sign, sb * sign


def _row_tile(seq, want):
    tm = min(want, seq)
    assert seq % tm == 0
    return tm


def kernel(x, attn_w_in, attn_w_out, ssm_w_in, ssm_a_re, ssm_a_im, ssm_log_dt, ssm_b_re, ssm_b_im,
           ssm_c_re, ssm_c_im, ssm_d, ssm_w_glu, ssm_b_glu, ssm_w_out, ffn_w_up, ffn_conv_w,
           ffn_conv_b, ffn_w_down, ln_g, ln_b):
    batch, seq, d_model = x.shape
    assert seq % DILATION_PAIRS[-1][0] == 0 and seq % SSM_CHUNK == 0
    assert all(w // d == ATTN_BLOCK for w, d in DILATION_PAIRS)
    h = x.reshape(batch * seq, d_model)
    ffn_chunk = 256

    w_up_all = ffn_w_up.astype(BF16)
    w_down_all = ffn_w_down.astype(BF16)

    def ffn(h, i, with_bf16_copy=False):
        return _conv_ffn(h, w_up_all, ffn_conv_w[i], ffn_conv_b[i], w_down_all, ln_g[i, 1], ln_b[i, 1],
                         i, seq, _row_tile(seq, 1024), ffn_chunk, with_bf16_copy)

    rope = _rope_tables(seq)
    w_in = attn_w_in[0].astype(BF16)
    os, lses = [], []
    for g, (_, dilation) in enumerate(DILATION_PAIRS):
        qkv_g = _qkv_rope(h, w_in, rope, g, dilation, batch, seq, _row_tile(seq, 1024))
        o, lse = _dilated_attention_group(qkv_g, g, 2048)
        os.append(o)
        lses.append(lse)
    h = _merge_out(os, lses, h, attn_w_out[0].astype(BF16), ln_g[0, 0], ln_b[0, 0], seq,
                   _row_tile(seq, 1024))
    h, h_bf16 = ffn(h, 0, with_bf16_copy=True)

    ssm_ops = _ssm_operators(
        ssm_a_re[0], ssm_a_im[0], ssm_log_dt[0], ssm_b_re[0], ssm_b_im[0], ssm_c_re[0], ssm_c_im[0])
    ut = _ssm_in(h_bf16, ssm_w_in[0].T.astype(BF16), _row_tile(seq, 1024))
    yt = _ssm_core(ut, *ssm_ops, seq // SSM_CHUNK)
    h = _ssm_out(yt, ut, ssm_d[0], ssm_w_glu[0].T.astype(BF16), ssm_b_glu[0],
                 ssm_w_out[0].astype(BF16), h, ln_g[1, 0], ln_b[1, 0], _row_tile(seq, 1024), 512)
    h = ffn(h, 1)
    return h.reshape(batch, seq, d_model)
```

```python
import functools
import math

import jax
import jax.numpy as jnp
from jax import lax
from jax.experimental import pallas as pl
from jax.experimental.pallas import tpu as pltpu

F32 = jnp.float32
BF16 = jnp.bfloat16

DEPTH = 2
DILATION_PAIRS = ((128, 1), (512, 4), (2048, 16))
N_GROUPS = len(DILATION_PAIRS)
HEADS = 8
HEAD_DIM = 128
ROPE_THETA = 10000.0
SSM_GROUP_CH = 16
SSM_STATE = 64
CONV_WIDTH = 3
DEEPNORM_ALPHA = (2.0 * DEPTH) ** 0.25
LN_EPS = 1e-5
NEG_INF = -1e30
LOG2E = math.log2(math.e)
LN2 = math.log(2.0)

LANES = 128
SUBLANES = 8
VMEM_LIMIT_BYTES = 56 * 1024 * 1024

ATTN_BLOCK = 128
ROPE_BLOCK = 128
PERM_ROWS = 256
SSM_CHUNK = 128
SSM_STEP_CH = 8


def _params(*semantics):
    return pltpu.CompilerParams(dimension_semantics=semantics,
                                vmem_limit_bytes=VMEM_LIMIT_BYTES)


def _layer_norm(r, g, b):
    mu = jnp.mean(r, axis=-1, keepdims=True)
    d = r - mu
    var = jnp.mean(d * d, axis=-1, keepdims=True)
    return d * lax.rsqrt(var + LN_EPS) * g + b


def _dot(a, b):
    return jnp.dot(a, b, preferred_element_type=F32)


def _dot_nt(a, b):
    return lax.dot_general(a, b, (((1,), (1,)), ((), ())), preferred_element_type=F32)


def _dot_tn(a, b):
    return lax.dot_general(a, b, (((0,), (0,)), ((), ())), preferred_element_type=F32)


def _residue_rows(r, n, d):
    return pl.ds(r, n, stride=d) if d > 1 else slice(None)


def _qkv_kernel(*refs, d):
    n_slabs = len(refs) - 14
    x_refs = refs[:n_slabs]
    (w_ref, ca_ref, sa_ref, cb_ref, sb_ref, cbs_ref, sbs_ref, o_ref,
     xb_ref, cos_ref, sin_ref, cs_ref, sn_ref, tmp_ref) = refs[n_slabs:]
    tm = cos_ref.shape[0]
    n = tm // d
    hw = HEADS * HEAD_DIM

    for q in range(tm // ROPE_BLOCK):
        blk = slice(q * ROPE_BLOCK, (q + 1) * ROPE_BLOCK)
        ca = ca_ref[q:q + 1, :]
        sa = sa_ref[q:q + 1, :]
        cos_ref[blk, :] = ca * cb_ref[...] - sa * sb_ref[...]
        sin_ref[blk, :] = sa * cbs_ref[...] + ca * sbs_ref[...]

    def residues(src_ref):
        if d % 8:
            return [src_ref[_residue_rows(r, n, d), :] for r in range(d)]
        for a in range(4):
            tmp_ref[a] = src_ref[pl.ds(a, tm // 4, stride=4), :]
        return [tmp_ref[r % 4, pl.ds(r // 4, n, stride=d // 4), :] for r in range(d)]

    for c, x_ref in enumerate(x_refs):
        for r, piece in enumerate(residues(x_ref)):
            xb_ref[r * n:(r + 1) * n, c * LANES:(c + 1) * LANES] = piece.astype(BF16)
    for src_ref, dst_ref in ((cos_ref, cs_ref), (sin_ref, sn_ref)):
        for r, piece in enumerate(residues(src_ref)):
            dst_ref[r * n:(r + 1) * n, :] = piece
    xb = xb_ref[...]

    for kind in range(3):
        acc = _dot(xb, w_ref[:, kind * hw:(kind + 1) * hw])
        if kind == 2:
            for r in range(d):
                o_ref[r, :, kind * hw:(kind + 1) * hw] = acc[r * n:(r + 1) * n, :].astype(BF16)
            continue
        scale = HEAD_DIM ** -0.5 * LOG2E if kind == 0 else 1.0
        cos = cs_ref[...] * scale
        sin = sn_ref[...] * scale
        for h in range(HEADS):
            t = acc[:, h * HEAD_DIM:(h + 1) * HEAD_DIM]
            rot = (t * cos + pltpu.roll(t, HEAD_DIM // 2, axis=1) * sin).astype(BF16)
            lanes = slice(kind * hw + h * HEAD_DIM, kind * hw + (h + 1) * HEAD_DIM)
            for r in range(d):
                o_ref[r, :, lanes] = rot[r * n:(r + 1) * n, :]


def _qkv_rope(x2d, w_in_bf16, rope, g, d, batch, seq, tm):
    d_model = x2d.shape[1]
    hw = HEADS * HEAD_DIM
    seq_tiles = seq // tm
    n = tm // d
    coarse = pl.BlockSpec((tm // ROPE_BLOCK, HEAD_DIM), lambda i: (i % seq_tiles, 0))
    fine = pl.BlockSpec((ROPE_BLOCK, HEAD_DIM), lambda i: (0, 0))
    return pl.pallas_call(
        functools.partial(_qkv_kernel, d=d),
        out_shape=jax.ShapeDtypeStruct((batch, d, seq // d, 3 * hw), BF16),
        grid=(batch * seq_tiles,),
        in_specs=[pl.BlockSpec((tm, LANES), lambda i, c=c: (i, c))
                  for c in range(d_model // LANES)] + [
            pl.BlockSpec((d_model, 3 * hw), lambda i: (0, g)),
            coarse, coarse, fine, fine, fine, fine,
        ],
        out_specs=pl.BlockSpec((None, d, n, 3 * hw),
                               lambda i: (i // seq_tiles, 0, i % seq_tiles, 0)),
        scratch_shapes=[pltpu.VMEM((tm, d_model), BF16),
                        pltpu.VMEM((tm, HEAD_DIM), F32),
                        pltpu.VMEM((tm, HEAD_DIM), F32),
                        pltpu.VMEM((tm, HEAD_DIM), F32),
                        pltpu.VMEM((tm, HEAD_DIM), F32),
                        pltpu.VMEM((4, tm // 4, LANES), F32)],
        compiler_params=_params("parallel"),
        name=f"qkv_rope_g{g}",
    )(*([x2d] * (d_model // LANES)), w_in_bf16, *rope)


def _attn_kernel(q_ref, k_ref, v_ref, kp_ref, vp_ref, o_ref, lse_ref, *, nblk):
    m = pl.program_id(2)
    blk_rows = ATTN_BLOCK
    row = lax.broadcasted_iota(jnp.int32, (blk_rows, 2 * blk_rows), 0)
    col = lax.broadcasted_iota(jnp.int32, (blk_rows, 2 * blk_rows), 1)
    band = jnp.logical_and(col >= row, col <= row + blk_rows)
    first_band = jnp.logical_and(band, jnp.logical_or(col >= blk_rows, m > 0))
    lane = lax.broadcasted_iota(jnp.int32, (blk_rows, LANES), 1)
    ones = jnp.ones((2 * blk_rows, HEAD_DIM), BF16)

    def block(ri, q_rows, keys, vals, mask):
        mx_tile = jnp.zeros((blk_rows, LANES), F32)
        l_tile = jnp.ones((blk_rows, LANES), F32)
        for h in range(HEADS):
            cols = slice(h * HEAD_DIM, (h + 1) * HEAD_DIM)
            s = jnp.where(mask, _dot_nt(q_ref[ri, q_rows, cols], keys(cols)), NEG_INF)
            mx = jnp.max(jnp.maximum(s[:, :blk_rows], s[:, blk_rows:]), axis=1, keepdims=True)
            p = jnp.exp2(s - mx).astype(BF16)
            oe = _dot(p, jnp.concatenate([vals(cols), ones], axis=1))
            l = oe[:, HEAD_DIM:]
            o_ref[ri, q_rows, cols] = (oe[:, :HEAD_DIM] / l).astype(BF16)
            mx_tile = jnp.where(lane == h, mx, mx_tile)
            l_tile = jnp.where(lane == h, l, l_tile)
        lse_ref[ri, q_rows, :] = mx_tile * LN2 + jnp.log(l_tile)

    head = slice(0, blk_rows)
    for ri in range(q_ref.shape[0]):
        block(ri, head,
              lambda cols: jnp.concatenate([kp_ref[ri, :, cols], k_ref[ri, head, cols]], axis=0),
              lambda cols: jnp.concatenate([vp_ref[ri, :, cols], v_ref[ri, head, cols]], axis=0),
              first_band)
        for blk in range(1, nblk):
            q_rows = slice(blk * blk_rows, (blk + 1) * blk_rows)
            kv_rows = slice((blk - 1) * blk_rows, (blk + 1) * blk_rows)
            block(ri, q_rows, lambda cols: k_ref[ri, kv_rows, cols],
                  lambda cols: v_ref[ri, kv_rows, cols], band)


def _dilated_attention_group(qkv_g, g, rows):
    batch, d, sub_len, _ = qkv_g.shape
    hw = HEADS * HEAD_DIM
    res = max(1, min(d, rows // sub_len))
    rows = min(rows, sub_len)
    nblk = rows // ATTN_BLOCK

    def main_spec(kind):
        return pl.BlockSpec((None, res, rows, hw), lambda b, r, m: (b, r, m, kind))

    def prev_spec(kind):
        return pl.BlockSpec((None, res, ATTN_BLOCK, hw),
                            lambda b, r, m: (b, r, jnp.maximum(m * nblk - 1, 0), kind))

    return pl.pallas_call(
        functools.partial(_attn_kernel, nblk=nblk),
        out_shape=(jax.ShapeDtypeStruct((batch, d, sub_len, hw), BF16),
                   jax.ShapeDtypeStruct((batch, d, sub_len, LANES), F32)),
        grid=(batch, d // res, sub_len // rows),
        in_specs=[main_spec(0), main_spec(1), main_spec(2), prev_spec(1), prev_spec(2)],
        out_specs=(pl.BlockSpec((None, res, rows, hw), lambda b, r, m: (b, r, m, 0)),
                   pl.BlockSpec((None, res, rows, LANES), lambda b, r, m: (b, r, m, 0))),
        compiler_params=_params("parallel", "parallel", "arbitrary"),
        name=f"dilated_attn_g{g}",
    )(qkv_g, qkv_g, qkv_g, qkv_g, qkv_g)


def _merge_out_kernel(o0_ref, o1_ref, o2_ref, l0_ref, l1_ref, l2_ref, e_ref, p1_ref, p2_ref, x_ref, w_ref,
                      g_ref, b_ref, out_ref, ln_ref):
    tm = x_ref.shape[0]
    groups = ((o0_ref, l0_ref, None), (o1_ref, l1_ref, p1_ref), (o2_ref, l2_ref, p2_ref))
    for gi, (o_ref, l_ref, _) in enumerate(groups):
        d = o_ref.shape[0]
        n = tm // d
        if d == 1:
            continue
        for r in range(d):
            ln_ref[gi, _residue_rows(r, n, d), :] = l_ref[r]
    e = e_ref[...]
    for h in range(tm // PERM_ROWS):
        rws = slice(h * PERM_ROWS, (h + 1) * PERM_ROWS)
        outs, lses = [], []
        for gi, (o_ref, l_ref, p_ref) in enumerate(groups):
            d = o_ref.shape[0]
            if d == 1:
                outs.append(o_ref[0, rws, :].astype(F32))
                lses.append(l_ref[0, rws, :])
                continue
            m = PERM_ROWS // d
            stacked = jnp.concatenate([o_ref[r, h * m:(h + 1) * m, :] for r in range(d)], axis=0)
            outs.append(_dot(p_ref[...], stacked))
            lses.append(ln_ref[gi, rws, :])
        mx = jnp.maximum(jnp.maximum(lses[0], lses[1]), lses[2])
        ws = [jnp.exp(l - mx) for l in lses]
        inv = 1.0 / (ws[0] + ws[1] + ws[2])
        merged = None
        for w, o in zip(ws, outs):
            w = w * inv
            w_hi = w.astype(BF16)
            w_lo = (w - w_hi.astype(F32)).astype(BF16)
            w_full = _dot(jnp.concatenate([w_hi, w_lo], axis=1), e)
            term = w_full * o
            merged = term if merged is None else merged + term
        y = _dot(merged.astype(BF16), w_ref[...])
        r = DEEPNORM_ALPHA * x_ref[rws, :] + y
        out_ref[rws, :] = _layer_norm(r, g_ref[...], b_ref[...])


def _token_order_matrix(d):
    t = jnp.arange(PERM_ROWS)
    p = (t % d) * (PERM_ROWS // d) + t // d
    return (p[:, None] == jnp.arange(PERM_ROWS)[None, :]).astype(BF16)


def _merge_out(os, lses, x2d, w_out_bf16, ln_g, ln_b, seq, tm):
    tokens, d_model = x2d.shape
    hw = HEADS * HEAD_DIM
    seq_tiles = seq // tm
    expand = (jnp.arange(2 * LANES)[:, None] % LANES == (jnp.arange(hw)[None, :] // HEAD_DIM)).astype(BF16)
    row = lambda i: (i, 0)
    fixed = lambda i: (0, 0)

    def group_spec(arr):
        d, width = arr.shape[1], arr.shape[3]
        return pl.BlockSpec((None, d, tm // d, width),
                            lambda i: (i // seq_tiles, 0, i % seq_tiles, 0))

    perms = [_token_order_matrix(a.shape[1]) for a in os[1:]]
    return pl.pallas_call(
        _merge_out_kernel,
        out_shape=jax.ShapeDtypeStruct((tokens, d_model), F32),
        grid=(tokens // tm,),
        in_specs=[group_spec(a) for a in os] + [group_spec(a) for a in lses] + [
            pl.BlockSpec((2 * LANES, hw), fixed),
            pl.BlockSpec((PERM_ROWS, PERM_ROWS), fixed),
            pl.BlockSpec((PERM_ROWS, PERM_ROWS), fixed),
            pl.BlockSpec((tm, d_model), row),
            pl.BlockSpec((hw, d_model), fixed),
            pl.BlockSpec((1, d_model), fixed),
            pl.BlockSpec((1, d_model), fixed),
        ],
        out_specs=pl.BlockSpec((tm, d_model), row),
        scratch_shapes=[pltpu.VMEM((N_GROUPS, tm, LANES), F32)],
        compiler_params=_params("parallel"),
        name="attn_merge_out_ln",
    )(*os, *lses, expand, *perms, x2d, w_out_bf16, ln_g.reshape(1, -1), ln_b.reshape(1, -1))


def _ffn_kernel(x_ref, halo_ref, wup_ref, cw_ref, cb_ref, wdown_ref, g_ref, b_ref, out_ref, *rest,
                seq_tiles, d_ff, chunk, ln_split):
    outb_ref = rest[0] if len(rest) == 3 else None
    xe_ref, h_ref = rest[-2:]
    i = pl.program_id(0)
    x = x_ref[...]
    first = (i % seq_tiles) == 0
    xe_ref[0:SUBLANES, :] = jnp.where(first, 0.0, halo_ref[...]).astype(BF16)
    xe_ref[SUBLANES:, :] = x.astype(BF16)
    xe = xe_ref[...]
    for c in range(d_ff // chunk):
        gcols = slice(c * chunk, (c + 1) * chunk)
        vcols = slice(d_ff + c * chunk, d_ff + (c + 1) * chunk)
        gate = _dot(xe, wup_ref[:, gcols])
        val = _dot(xe[SUBLANES:, :], wup_ref[:, vcols])
        conv = (cb_ref[:, gcols]
                + cw_ref[0:1, gcols] * pltpu.roll(gate, 2, axis=0)[SUBLANES:, :]
                + cw_ref[1:2, gcols] * pltpu.roll(gate, 1, axis=0)[SUBLANES:, :]
                + cw_ref[2:3, gcols] * gate[SUBLANES:, :])
        act = conv * (1.0 / (1.0 + jnp.exp(-conv)))
        h_ref[:, gcols] = (act * val).astype(BF16)
    part = x_ref.shape[0] // ln_split
    for s in range(ln_split):
        rows = slice(s * part, (s + 1) * part)
        f = _dot(h_ref[rows, :], wdown_ref[...])
        r = DEEPNORM_ALPHA * x_ref[rows, :] + f
        y = _layer_norm(r, g_ref[...], b_ref[...])
        out_ref[rows, :] = y
        if outb_ref is not None:
            outb_ref[rows, :] = y.astype(BF16)


def _conv_ffn(x2d, w_up_bf16, conv_w, conv_b, w_down_bf16, ln_g, ln_b, layer, seq, tm, chunk,
              with_bf16_copy=False):
    tokens, d_model = x2d.shape
    d_ff = w_down_bf16.shape[1]
    seq_tiles = seq // tm
    halo_blocks = tm // SUBLANES
    fixed = lambda i: (0, 0)
    of_layer = lambda i: (layer, 0, 0)
    single = dict(pipeline_mode=pl.Buffered(1))
    in_specs = [
        pl.BlockSpec((tm, d_model), lambda i: (i, 0)),
        pl.BlockSpec((SUBLANES, d_model), lambda i: (jnp.maximum(i * halo_blocks - 1, 0), 0)),
        pl.BlockSpec((None, d_model, 2 * d_ff), of_layer, **single),
        pl.BlockSpec((CONV_WIDTH, d_ff), fixed),
        pl.BlockSpec((1, d_ff), fixed),
        pl.BlockSpec((None, d_ff, d_model), of_layer, **single),
        pl.BlockSpec((1, d_model), fixed),
        pl.BlockSpec((1, d_model), fixed),
    ]
    operands = [x2d, x2d, w_up_bf16, conv_w, conv_b.reshape(1, -1), w_down_bf16,
                ln_g.reshape(1, -1), ln_b.reshape(1, -1)]
    out_shape = jax.ShapeDtypeStruct((tokens, d_model), F32)
    out_specs = pl.BlockSpec((tm, d_model), lambda i: (i, 0))
    if with_bf16_copy:
        out_shape = (out_shape, jax.ShapeDtypeStruct((tokens, d_model), BF16))
        out_specs = (out_specs, pl.BlockSpec((tm, d_model), lambda i: (i, 0)))
    return pl.pallas_call(
        functools.partial(_ffn_kernel, seq_tiles=seq_tiles, d_ff=d_ff, chunk=chunk, ln_split=4),
        out_shape=out_shape,
        grid=(tokens // tm,),
        in_specs=in_specs,
        out_specs=out_specs,
        scratch_shapes=[pltpu.VMEM((tm + SUBLANES, d_model), BF16),
                        pltpu.VMEM((tm, d_ff), BF16)],
        compiler_params=_params("parallel"),
        name="conv_ffn_ln",
    )(*operands)


def _ssm_in_kernel(x_ref, wt_ref, ut_ref):
    gc = SSM_GROUP_CH
    u = _dot_nt(wt_ref[...], x_ref[...])
    for g in range(ut_ref.shape[0]):
        for q in range(ut_ref.shape[1] // gc):
            ut_ref[g, q * gc:(q + 1) * gc, :] = u[g * gc:(g + 1) * gc, q * SSM_CHUNK:(q + 1) * SSM_CHUNK]


def _ssm_in(x2d_bf16, w_in_t_bf16, tn):
    tokens, d_model = x2d_bf16.shape
    d_ssm = w_in_t_bf16.shape[0]
    gc = SSM_GROUP_CH
    groups = d_ssm // gc
    return pl.pallas_call(
        _ssm_in_kernel,
        out_shape=jax.ShapeDtypeStruct((groups, tokens // SSM_CHUNK * gc, SSM_CHUNK), F32),
        grid=(tokens // tn,),
        in_specs=[pl.BlockSpec((tn, d_model), lambda i: (i, 0)),
                  pl.BlockSpec((d_ssm, d_model), lambda i: (0, 0))],
        out_specs=pl.BlockSpec((groups, tn // SSM_CHUNK * gc, SSM_CHUNK), lambda i: (0, i, 0)),
        compiler_params=_params("parallel"),
        name="ssm_in_proj",
    )(x2d_bf16, w_in_t_bf16)


def _split_bf16(x):
    hi = x.astype(BF16)
    return hi, (x - hi.astype(F32)).astype(BF16)


def _ssm_core_kernel(ut_ref, rpt_ref, fpt_ref, ba_ref, bb_ref, ca_ref, cb_ref, la_ref, lb_ref,
                     ct_ref, pw_ref, ctn_ref, ban_ref, bbn_ref, pwn_ref,
                     yt_ref, toep0_ref, toep1_ref, kv_ref, cbs_ref, pm_ref, cmt_ref, yacc_ref, st_ref,
                     tmp_ref, *, chunks_per_seq):
    gc = SSM_GROUP_CH
    L = SSM_CHUNK
    rows = ut_ref.shape[0] // gc
    n2 = 2 * SSM_STATE
    per_step = SSM_STEP_CH
    pair = per_step * L
    steps = gc // per_step
    g = pl.program_id(0)
    slot = g % 2

    srow = lax.broadcasted_iota(jnp.int32, (L, L), 0)
    tcol = lax.broadcasted_iota(jnp.int32, (L, L), 1)
    causal = tcol >= srow
    conj = jnp.where(lax.broadcasted_iota(jnp.int32, (1, n2), 1) < SSM_STATE, 1.0, -1.0)

    def taps(c_ref, a_ref, b_ref, p_ref):
        ct = c_ref[...]
        ct_sw = pltpu.roll(ct, SSM_STATE, axis=1)
        for c in range(gc):
            cb_c = ct * a_ref[c:c + 1, :] + ct_sw * b_ref[c:c + 1, :]
            cbs_ref[c * gc:(c + 1) * gc, :] = cb_c * conj
        m_hi, m_lo = _split_bf16(cbs_ref[...])
        p_hi, p_lo = _split_bf16(p_ref[...])
        kv_ref[...] = _dot(m_hi, p_hi) + (_dot(m_hi, p_lo) + _dot(m_lo, p_hi))

    def build_pair(i, dst_ref):
        for cc in range(per_step):
            c = per_step * i + cc
            for cp in range(gc):
                kv = kv_ref[pl.ds(c * gc + cp, 1), :]
                blk = pltpu.roll(jnp.broadcast_to(kv, (L, L)), 0, 1, stride=1, stride_axis=0)
                blk = jnp.where(causal, blk, 0.0)
                dst_ref[pl.ds(pl.multiple_of(c * L, L), L), cp * L:(cp + 1) * L] = blk.astype(BF16)

    @pl.when(g == 0)
    def _():
        taps(ct_ref, ba_ref, bb_ref, pw_ref)

        def first(i, carry):
            build_pair(i, toep0_ref)
            return carry
        lax.fori_loop(0, steps, first, 0)

    taps(ctn_ref, ban_ref, bbn_ref, pwn_ref)

    quarter = rows * gc // 4
    for a in range(4):
        tmp_ref[a] = ut_ref[pl.ds(a, quarter, stride=4), :]

    rpt = rpt_ref[...]
    rpt_sw = pltpu.roll(rpt, SSM_STATE, axis=1)
    fpt = fpt_ref[...]
    fpt_sw = pltpu.roll(fpt, SSM_STATE, axis=1)
    st_ref[...] = jnp.zeros_like(st_ref)
    yacc_ref[...] = jnp.zeros_like(yacc_ref)

    def pipeline(cur_ref, nxt_ref):
        def step(i, carry):
            build_pair(i, nxt_ref)
            halves = []
            for cc in range(per_step):
                c = per_step * i + cc
                halves.append(tmp_ref[c % 4, pl.ds(c // 4, rows, stride=4), :].astype(BF16))
                crow = pl.ds(c, 1)
                pm_ref[cc * L:(cc + 1) * L, :] = (rpt * ba_ref[crow, :] + rpt_sw * bb_ref[crow, :]).astype(BF16)
                cmt_ref[pl.ds(pl.multiple_of(c * L, L), L), :] = (
                    fpt * ca_ref[crow, :] + fpt_sw * cb_ref[crow, :]).astype(BF16)
            lhs = jnp.concatenate(halves, axis=1)
            rws = pl.ds(pl.multiple_of(i * pair, pair), pair)
            yacc_ref[...] += _dot(lhs, cur_ref[rws, :])
            st_ref[...] += _dot(lhs, pm_ref[...])
            return carry
        lax.fori_loop(0, steps, step, 0)

    @pl.when(slot == 0)
    def _():
        pipeline(toep0_ref, toep1_ref)

    @pl.when(slot == 1)
    def _():
        pipeline(toep1_ref, toep0_ref)

    st = st_ref[...]
    a = la_ref[...]
    b = lb_ref[...]
    jrow = lax.broadcasted_iota(jnp.int32, (rows, n2), 0) % chunks_per_seq
    sh = 1
    while sh < chunks_per_seq:
        prev = jnp.where(jrow >= sh, pltpu.roll(st, sh, axis=0), 0.0)
        st = st + prev * a + pltpu.roll(prev, SSM_STATE, axis=1) * b
        a, b = a * a - b * b, 2.0 * a * b
        sh *= 2
    carried = jnp.where(jrow >= 1, pltpu.roll(st, 1, axis=0), 0.0)
    yacc_ref[...] += _dot_nt(carried.astype(BF16), cmt_ref[...])

    for cp in range(gc):
        tmp_ref[cp % 4, pl.ds(cp // 4, rows, stride=4), :] = yacc_ref[:, cp * L:(cp + 1) * L]
    for a4 in range(4):
        yt_ref[pl.ds(a4, quarter, stride=4), :] = tmp_ref[a4]


def _ssm_core(ut, rpt, fpt, ba, bb, ca, cb, lam_a, lam_b, ctile, pw, chunks_per_seq):
    groups, group_rows, L = ut.shape
    gc, n2 = SSM_GROUP_CH, 2 * SSM_STATE
    rows = group_rows // gc
    per_group = lambda g: (g, 0, 0)
    next_group = lambda g: (jnp.minimum(g + 1, groups - 1), 0, 0)
    return pl.pallas_call(
        functools.partial(_ssm_core_kernel, chunks_per_seq=chunks_per_seq),
        out_shape=jax.ShapeDtypeStruct((groups, group_rows, L), F32),
        grid=(groups,),
        in_specs=[
            pl.BlockSpec((None, group_rows, L), per_group),
            pl.BlockSpec((None, L, n2), per_group),
            pl.BlockSpec((None, L, n2), per_group),
            pl.BlockSpec((None, gc, n2), per_group),
            pl.BlockSpec((None, gc, n2), per_group),
            pl.BlockSpec((None, gc, n2), per_group),
            pl.BlockSpec((None, gc, n2), per_group),
            pl.BlockSpec((None, 1, n2), per_group),
            pl.BlockSpec((None, 1, n2), per_group),
            pl.BlockSpec((None, gc, n2), per_group),
            pl.BlockSpec((None, n2, L), per_group),
            pl.BlockSpec((None, gc, n2), next_group),
            pl.BlockSpec((None, gc, n2), next_group),
            pl.BlockSpec((None, gc, n2), next_group),
            pl.BlockSpec((None, n2, L), next_group),
        ],
        out_specs=pl.BlockSpec((None, group_rows, L), per_group),
        scratch_shapes=[pltpu.VMEM((gc * L, gc * L), BF16),
                        pltpu.VMEM((gc * L, gc * L), BF16),
                        pltpu.VMEM((gc * gc, L), F32),
                        pltpu.VMEM((gc * gc, n2), F32),
                        pltpu.VMEM((SSM_STEP_CH * L, n2), BF16),
                        pltpu.VMEM((gc * L, n2), BF16),
                        pltpu.VMEM((rows, gc * L), F32),
                        pltpu.VMEM((rows, n2), F32),
                        pltpu.VMEM((4, rows * gc // 4, L), F32)],
        compiler_params=_params("arbitrary"),
        name="ssm_chunk_conv",
    )(ut, rpt, fpt, ba, bb, ca, cb, lam_a, lam_b, ctile, pw, ctile, ba, bb, pw)


def _ssm_operators(a_re, a_im, log_dt, b_re, b_im, c_re, c_im):
    L = SSM_CHUNK
    a_re = a_re.astype(F32)
    a_im = a_im.astype(F32)
    dt = jnp.exp(log_dt.astype(F32))[:, None]
    mag = jnp.exp(a_re * dt)
    lam_re = mag * jnp.cos(a_im * dt)
    lam_im = mag * jnp.sin(a_im * dt)
    nr, ni = lam_re - 1.0, lam_im
    den = a_re * a_re + a_im * a_im
    coef_re = ((nr * a_re + ni * a_im) / den)[..., None]
    coef_im = ((ni * a_re - nr * a_im) / den)[..., None]
    b_re = b_re.astype(F32)
    b_im = b_im.astype(F32)
    bb_re = coef_re * b_re - coef_im * b_im
    bb_im = coef_re * b_im + coef_im * b_re
    c_re = c_re.astype(F32)
    c_im = c_im.astype(F32)

    k = jnp.arange(L + 1, dtype=F32)
    pmag = jnp.exp((a_re * dt)[..., None] * k)
    ang = (a_im * dt)[..., None] * k
    pw_re = pmag * jnp.cos(ang)
    pw_im = pmag * jnp.sin(ang)

    def lanes(re, im):
        return jnp.concatenate([jnp.swapaxes(re, 1, 2), jnp.swapaxes(im, 1, 2)], axis=-1)

    rpt = lanes(jnp.flip(pw_re[..., :L], axis=-1), jnp.flip(pw_im[..., :L], axis=-1))
    fpt = lanes(pw_re[..., 1:], pw_im[..., 1:])
    ba = lanes(bb_re, bb_re)
    bb = lanes(-bb_im, bb_im)
    ca = jnp.concatenate([c_re, -c_re], axis=-1)
    cb = jnp.concatenate([-c_im, -c_im], axis=-1)
    lam_a = jnp.concatenate([pw_re[..., L], pw_re[..., L]], axis=-1)[:, None, :]
    lam_b = jnp.concatenate([-pw_im[..., L], pw_im[..., L]], axis=-1)[:, None, :]
    ctile = jnp.concatenate([c_re, c_im], axis=-1)
    pw = jnp.concatenate([pw_re[..., :L], pw_im[..., :L]], axis=1)
    return rpt, fpt, ba, bb, ca, cb, lam_a, lam_b, ctile, pw


def _ssm_out_kernel(yt_ref, ut_ref, d_ref, wg_ref, bg_ref, wo_ref, x_ref, g_ref, b_ref, out_ref, *, sub):
    gc = SSM_GROUP_CH
    groups, nq = yt_ref.shape[0], yt_ref.shape[1] // gc
    qs = sub // SSM_CHUNK
    n_sub = nq // qs

    def channel_major(ref, q0):
        return jnp.concatenate(
            [jnp.concatenate([ref[g, q * gc:(q + 1) * gc, :] for g in range(groups)], axis=0)
             for q in range(q0, q0 + qs)], axis=1)

    def gelu_gate(s):
        y = channel_major(yt_ref, s * qs) + d_ref[...] * channel_major(ut_ref, s * qs)
        z = 0.5 * y * (1.0 + jnp.tanh(math.sqrt(2.0 / math.pi) * (y + 0.044715 * (y * y * y))))
        return z, _dot(wg_ref[...], z.astype(BF16)) + bg_ref[...]

    def glu_out(s, z, gate):
        rows = slice(s * sub, (s + 1) * sub)
        zs = z * (1.0 / (1.0 + jnp.exp(-gate)))
        f = _dot_tn(zs.astype(BF16), wo_ref[...])
        r = DEEPNORM_ALPHA * x_ref[rows, :] + f
        out_ref[rows, :] = _layer_norm(r, g_ref[...], b_ref[...])

    pending = gelu_gate(0)
    for s in range(n_sub):
        upcoming = gelu_gate(s + 1) if s + 1 < n_sub else None
        glu_out(s, *pending)
        pending = upcoming


def _ssm_out(yt, ut, d_skip, w_glu_t_bf16, b_glu, w_out_bf16, x2d, ln_g, ln_b, tn, sub):
    tokens, d_model = x2d.shape
    groups = yt.shape[0]
    d_ssm = groups * SSM_GROUP_CH
    fixed = lambda i: (0, 0)
    chunked = pl.BlockSpec((groups, tn // SSM_CHUNK * SSM_GROUP_CH, SSM_CHUNK), lambda i: (0, i, 0))
    return pl.pallas_call(
        functools.partial(_ssm_out_kernel, sub=sub),
        out_shape=jax.ShapeDtypeStruct((tokens, d_model), F32),
        grid=(tokens // tn,),
        in_specs=[
            chunked,
            chunked,
            pl.BlockSpec((d_ssm, 1), fixed),
            pl.BlockSpec((d_ssm, d_ssm), fixed),
            pl.BlockSpec((d_ssm, 1), fixed),
            pl.BlockSpec((d_ssm, d_model), fixed),
            pl.BlockSpec((tn, d_model), lambda i: (i, 0)),
            pl.BlockSpec((1, d_model), fixed),
            pl.BlockSpec((1, d_model), fixed),
        ],
        out_specs=pl.BlockSpec((tn, d_model), lambda i: (i, 0)),
        compiler_params=_params("parallel"),
        name="ssm_glu_out_ln",
    )(yt, ut, d_skip.reshape(-1, 1), w_glu_t_bf16, b_glu.reshape(-1, 1), w_out_bf16, x2d,
      ln_g.reshape(1, -1), ln_b.reshape(1, -1))


def _rope_tables(seq):
    inv_freq = ROPE_THETA ** (-jnp.arange(0, HEAD_DIM, 2, dtype=F32) / HEAD_DIM)
    inv_freq = jnp.concatenate([inv_freq, inv_freq])[None, :]
    ang_a = (jnp.arange(seq // ROPE_BLOCK, dtype=F32) * ROPE_BLOCK)[:, None] * inv_freq
    ang_b = jnp.arange(ROPE_BLOCK, dtype=F32)[:, None] * inv_freq
    sign = jnp.where(jnp.arange(HEAD_DIM) < HEAD_DIM // 2, -1.0, 1.0).astype(F32)[None, :]
    cb, sb = jnp.cos(ang_b), jnp.sin(ang_b)
    return jnp.cos(ang_a), jnp.sin(ang_a), cb, sb, cb * sign, sb * sign


def _row_tile(seq, want):
    tm = min(want, seq)
    assert seq % tm == 0
    return tm


def kernel(x, attn_w_in, attn_w_out, ssm_w_in, ssm_a_re, ssm_a_im, ssm_log_dt, ssm_b_re, ssm_b_im,
           ssm_c_re, ssm_c_im, ssm_d, ssm_w_glu, ssm_b_glu, ssm_w_out, ffn_w_up, ffn_conv_w,
           ffn_conv_b, ffn_w_down, ln_g, ln_b):
    batch, seq, d_model = x.shape
    assert seq % DILATION_PAIRS[-1][0] == 0 and seq % SSM_CHUNK == 0
    assert all(w // d == ATTN_BLOCK for w, d in DILATION_PAIRS)
    h = x.reshape(batch * seq, d_model)
    ffn_chunk = 256

    w_up_all = ffn_w_up.astype(BF16)
    w_down_all = ffn_w_down.astype(BF16)

    def ffn(h, i, with_bf16_copy=False):
        return _conv_ffn(h, w_up_all, ffn_conv_w[i], ffn_conv_b[i], w_down_all, ln_g[i, 1], ln_b[i, 1],
                         i, seq, _row_tile(seq, 1024), ffn_chunk, with_bf16_copy)

    rope = _rope_tables(seq)
    w_in = attn_w_in[0].astype(BF16)
    os, lses = [], []
    for g, (_, dilation) in enumerate(DILATION_PAIRS):
        qkv_g = _qkv_rope(h, w_in, rope, g, dilation, batch, seq, _row_tile(seq, 1024))
        o, lse = _dilated_attention_group(qkv_g, g, 2048)
        os.append(o)
        lses.append(lse)
    h = _merge_out(os, lses, h, attn_w_out[0].astype(BF16), ln_g[0, 0], ln_b[0, 0], seq,
                   _row_tile(seq, 1024))
    h, h_bf16 = ffn(h, 0, with_bf16_copy=True)

    ssm_ops = _ssm_operators(
        ssm_a_re[0], ssm_a_im[0], ssm_log_dt[0], ssm_b_re[0], ssm_b_im[0], ssm_c_re[0], ssm_c_im[0])
    ut = _ssm_in(h_bf16, ssm_w_in[0].T.astype(BF16), _row_tile(seq, 1024))
    yt = _ssm_core(ut, *ssm_ops, seq // SSM_CHUNK)
    h = _ssm_out(yt, ut, ssm_d[0], ssm_w_glu[0].T.astype(BF16), ssm_b_glu[0],
                 ssm_w_out[0].astype(BF16), h, ln_g[1, 0], ln_b[1, 0], _row_tile(seq, 1024), 512)
    h = ffn(h, 1)
    return h.reshape(batch, seq, d_model)
```
